```python
import jax, jax.numpy as jnp
from jax import lax
import numpy as np

D_MODEL = 1024
BATCH = 8
SEQ = 2048
DEPTH = 4
DEC_BATCH = 128
DEC_SEQ = 4
PAST_LEN = 16384
PAGE_SIZE = 128

N_AB = (DEPTH + 1) // 2
N_C = DEPTH // 2
D_POOL = D_MODEL // 2
POOL_WINDOWS = (2, 4, 8, 16)
N_POOL_GROUPS = len(POOL_WINDOWS)
POOL_GROUP = D_POOL // N_POOL_GROUPS
POOL_HIST = max(POOL_WINDOWS) - 1
D_GMLP = D_MODEL // 2
GMLP_CHUNK = 128
H_GMLP = 4
GMLP_HEAD = D_GMLP // H_GMLP
D_IN_AB = D_POOL + 2 * D_GMLP
D_MIX_AB = D_POOL + D_GMLP
D_CONV = D_MODEL
CONV_WIDTH = 31
CONV_HIST = CONV_WIDTH - 1
D_FF = 4 * D_MODEL
EPS = 1e-6

kernel_name = 'pool_gmlp_conformer_hybrid_step'


def rmsnorm(x, g):
    xf = x.astype(jnp.float32)
    y = xf * lax.rsqrt(jnp.mean(xf * xf, axis=-1, keepdims=True) + EPS)
    return (y * g.astype(jnp.float32)).astype(x.dtype)


def layernorm(x, g, b):
    xf = x.astype(jnp.float32)
    mu = jnp.mean(xf, axis=-1, keepdims=True)
    var = jnp.mean(jnp.square(xf - mu), axis=-1, keepdims=True)
    y = (xf - mu) * lax.rsqrt(var + EPS)
    return (y * g.astype(jnp.float32) + b.astype(jnp.float32)).astype(x.dtype)


def multiscale_pool(a, hist, p0, w_grp, scale):
    B, L, _ = a.shape
    ext = jnp.concatenate([hist.astype(a.dtype), a], axis=1)
    cs = jnp.concatenate([jnp.zeros((B, 1, D_POOL), jnp.float32),
                          jnp.cumsum(ext.astype(jnp.float32), axis=1)], axis=1)
    end = cs[:, POOL_HIST + 1:]
    pos = p0 + jnp.arange(L, dtype=jnp.int32)
    means = []
    for g, w in enumerate(POOL_WINDOWS):
        sl = slice(g * POOL_GROUP, (g + 1) * POOL_GROUP)
        start = cs[:, POOL_HIST + 1 - w:POOL_HIST + 1 - w + L, sl]
        cnt = jnp.minimum(pos + 1, w).astype(jnp.float32)[None, :, None]
        means.append((end[..., sl] - start) / cnt)
    pooled = jnp.stack(means, axis=2) - a.astype(jnp.float32).reshape(B, L, N_POOL_GROUPS, POOL_GROUP)
    mixed = jnp.einsum('blgc,gcd->blgd', pooled.astype(a.dtype), w_grp).reshape(B, L, D_POOL)
    return mixed * scale, ext[:, -POOL_HIST:]


def chunk_spatial_gate(v, w_s, b_s):
    B, L, _ = v.shape
    n_chunks = -(-L // GMLP_CHUNK)
    Lp = n_chunks * GMLP_CHUNK
    vp = jnp.pad(v, ((0, 0), (0, Lp - L), (0, 0))).reshape(B, n_chunks, GMLP_CHUNK, H_GMLP, GMLP_HEAD)
    mask = jnp.tril(jnp.ones((GMLP_CHUNK, GMLP_CHUNK), dtype=bool))
    wm = jnp.where(mask[None], w_s, 0).astype(v.dtype)
    mixed = jnp.einsum('hts,bnshd->bnthd', wm, vp) + jnp.transpose(b_s)[None, None, :, :, None].astype(v.dtype)
    return mixed.reshape(B, Lp, D_GMLP)[:, :L]


def causal_dwconv(g, hist, w_dw, b_dw):
    ext = jnp.concatenate([hist.astype(g.dtype), g], axis=1)
    y = lax.conv_general_dilated(ext, w_dw[:, None, :].astype(g.dtype), window_strides=(1,),
                                 padding='VALID', dimension_numbers=('NWC', 'WIO', 'NWC'),
                                 feature_group_count=D_CONV)
    return y + b_dw.astype(g.dtype), ext[:, -CONV_HIST:]


def pool_gate_layer(x, hist, p0, g_norm, w_in, pool_w, pool_scale, ln_g, ln_b, w_s, b_s, w_out):
    h = rmsnorm(x, g_norm)
    z = h @ w_in
    a = z[..., :D_POOL]
    uv = jax.nn.gelu(z[..., D_POOL:], approximate=False)
    u, v = uv[..., :D_GMLP], uv[..., D_GMLP:]
    v = layernorm(v, ln_g, ln_b)
    pool_out, new_hist = multiscale_pool(a, hist, p0, pool_w, pool_scale)
    gate_out = u * chunk_spatial_gate(v, w_s, b_s)
    out = jnp.concatenate([pool_out, gate_out], axis=-1) @ w_out
    return x + out, new_hist, v


def conformer_conv_layer(x, hist, g_norm, w_pw1, w_dw, b_dw, ln_g, ln_b, w_pw2):
    h = rmsnorm(x, g_norm)
    z = h @ w_pw1
    glu = z[..., :D_CONV] * jax.nn.sigmoid(z[..., D_CONV:])
    c, new_hist = causal_dwconv(glu, hist, w_dw, b_dw)
    c = jax.nn.silu(layernorm(c, ln_g, ln_b))
    return x + c @ w_pw2, new_hist


def sqrelu_ffn(x, g_norm, w1, w2):
    h = rmsnorm(x, g_norm)
    return x + jnp.square(jax.nn.relu(h @ w1)) @ w2


def setup_inputs(seed: int = 0) -> dict:
    key = jax.random.key(seed)
    ks = jax.random.split(key, 24)
    f32 = jnp.float32

    def nrm(k, shape, s):
        return s * jax.random.normal(k, shape, f32)

    return {
        'x_prompt': nrm(ks[0], (BATCH, SEQ, D_MODEL), 1.0),
        'x_sample': nrm(ks[1], (DEC_BATCH, DEC_SEQ, D_MODEL), 1.0),
        'state_pool': nrm(ks[2], (N_AB, DEC_BATCH, POOL_HIST, D_POOL), 1.0),
        'state_conv': nrm(ks[3], (N_C, DEC_BATCH, CONV_HIST, D_CONV), 0.5),
        'norm_mix': 1.0 + nrm(ks[4], (DEPTH, D_MODEL), 0.02),
        'norm_ffn': 1.0 + nrm(ks[5], (DEPTH, D_MODEL), 0.02),
        'norm_final': 1.0 + nrm(ks[6], (D_MODEL,), 0.02),
        'ab_w_in': nrm(ks[7], (N_AB, D_MODEL, D_IN_AB), D_MODEL ** -0.5),
        'ab_pool_w': nrm(ks[8], (N_AB, N_POOL_GROUPS, POOL_GROUP, POOL_GROUP), POOL_GROUP ** -0.5),
        'ab_pool_scale': 1.0 + nrm(ks[9], (N_AB, D_POOL), 0.02),
        'ab_ln_g': 1.0 + nrm(ks[10], (N_AB, D_GMLP), 0.02),
        'ab_ln_b': nrm(ks[11], (N_AB, D_GMLP), 0.02),
        'ab_ws': nrm(ks[12], (N_AB, H_GMLP, GMLP_CHUNK, GMLP_CHUNK), GMLP_CHUNK ** -0.5),
        'ab_bs': 1.0 + nrm(ks[13], (N_AB, H_GMLP, GMLP_CHUNK), 0.02),
        'ab_w_out': nrm(ks[14], (N_AB, D_MIX_AB, D_MODEL), D_MIX_AB ** -0.5),
        'c_w_pw1': nrm(ks[15], (N_C, D_MODEL, 2 * D_CONV), D_MODEL ** -0.5),
        'c_w_dw': nrm(ks[16], (N_C, CONV_WIDTH, D_CONV), CONV_WIDTH ** -0.5),
        'c_b_dw': nrm(ks[17], (N_C, D_CONV), 0.02),
        'c_ln_g': 1.0 + nrm(ks[18], (N_C, D_CONV), 0.02),
        'c_ln_b': nrm(ks[19], (N_C, D_CONV), 0.02),
        'c_w_pw2': nrm(ks[20], (N_C, D_CONV, D_MODEL), D_CONV ** -0.5),
        'ffn_w1': nrm(ks[21], (DEPTH, D_MODEL, D_FF), D_MODEL ** -0.5),
        'ffn_w2': nrm(ks[22], (DEPTH, D_FF, D_MODEL), D_FF ** -0.5),
    }


def reference(x_prompt, x_sample, state_pool, state_conv, norm_mix, norm_ffn, norm_final,
              ab_w_in, ab_pool_w, ab_pool_scale, ab_ln_g, ab_ln_b, ab_ws, ab_bs, ab_w_out,
              c_w_pw1, c_w_dw, c_b_dw, c_ln_g, c_ln_b, c_w_pw2, ffn_w1, ffn_w2):
    xp, xs = x_prompt, x_sample
    bp = xp.shape[0]
    pool_p, pool_s, conv_p, conv_s, gate_v_s = [], [], [], [], []
    for layer in range(DEPTH):
        i = layer // 2
        if layer % 2 == 0:
            w = (norm_mix[layer], ab_w_in[i], ab_pool_w[i], ab_pool_scale[i], ab_ln_g[i], ab_ln_b[i],
                 ab_ws[i], ab_bs[i], ab_w_out[i])
            xp, hp, _ = pool_gate_layer(xp, jnp.zeros((bp, POOL_HIST, D_POOL), xp.dtype), 0, *w)
            xs, hs, vs = pool_gate_layer(xs, state_pool[i], PAST_LEN, *w)
            pool_p.append(hp)
            pool_s.append(hs)
            gate_v_s.append(vs)
        else:
            w = (norm_mix[layer], c_w_pw1[i], c_w_dw[i], c_b_dw[i], c_ln_g[i], c_ln_b[i], c_w_pw2[i])
            xp, hp = conformer_conv_layer(xp, jnp.zeros((bp, CONV_HIST, D_CONV), xp.dtype), *w)
            xs, hs = conformer_conv_layer(xs, state_conv[i], *w)
            conv_p.append(hp)
            conv_s.append(hs)
        xp = sqrelu_ffn(xp, norm_ffn[layer], ffn_w1[layer], ffn_w2[layer])
        xs = sqrelu_ffn(xs, norm_ffn[layer], ffn_w1[layer], ffn_w2[layer])
    y_prompt = rmsnorm(xp, norm_final)
    y_sample = rmsnorm(xs, norm_final)
    return (y_prompt, y_sample, jnp.stack(pool_p), jnp.stack(pool_s), jnp.stack(conv_p), jnp.stack(conv_s), jnp.stack(gate_v_s))
```

```python
import functools

import jax
import jax.numpy as jnp
from jax import lax
from jax.experimental import pallas as pl
from jax.experimental.pallas import tpu as pltpu

EPS = 1e-6
POOL_WINDOWS = (2, 4, 8, 16)
POOL_HIST = max(POOL_WINDOWS) - 1
GMLP_CHUNK = 128
CONV_WIDTH = 31
CONV_HIST = CONV_WIDTH - 1
PAST_LEN = 16384

LANES = 128
SUBLANES = 8
VMEM_LIMIT_BYTES = 56 * 1024 * 1024
ROW_TILE = 512
CONV_ROW_BLOCK = 8

F32 = jnp.float32
BF16 = jnp.bfloat16


def _rmsnorm(x, g):
    return x * lax.rsqrt(jnp.mean(x * x, axis=-1, keepdims=True) + EPS) * g


def _layernorm(x, g, b):
    mu = jnp.mean(x, axis=-1, keepdims=True)
    xc = x - mu
    var = jnp.mean(xc * xc, axis=-1, keepdims=True)
    return xc * lax.rsqrt(var + EPS) * g + b


def _gelu(x):
    return 0.5 * x * (1.0 + lax.erf(x * (2.0 ** -0.5)))


def _dot(a, b):
    return jnp.dot(a, b, preferred_element_type=F32)


def _resident(shape):
    zeros = (0,) * len(shape)
    return pl.BlockSpec(shape, lambda *_: zeros, pipeline_mode=pl.Buffered(1))


def _params(n_grid_dims):
    return pltpu.CompilerParams(dimension_semantics=("arbitrary",) * n_grid_dims,
                                vmem_limit_bytes=VMEM_LIMIT_BYTES)


def _ffn_kernel(x_ref, g_ref, w1_ref, w2_ref, gf_ref, o_ref, *, final):
    x = x_ref[...]
    h = _rmsnorm(x, g_ref[...]).astype(BF16)
    a = _dot(h, w1_ref[...])
    a = jnp.square(jnp.maximum(a, 0.0)).astype(BF16)
    y = x + _dot(a, w2_ref[...])
    if final:
        y = _rmsnorm(y, gf_ref[...])
    o_ref[...] = y


def _ffn(x, g, w1, w2, gf, *, final, tm):
    rows, d = x.shape
    dff = w1.shape[1]
    return pl.pallas_call(
        functools.partial(_ffn_kernel, final=final),
        grid=(rows // tm,),
        in_specs=[pl.BlockSpec((tm, d), lambda i: (i, 0)),
                  _resident((1, d)), _resident((d, dff)), _resident((dff, d)), _resident((1, d))],
        out_specs=pl.BlockSpec((tm, d), lambda i: (i, 0)),
        out_shape=jax.ShapeDtypeStruct((rows, d), F32),
        compiler_params=_params(1),
        name="ffn",
    )(x, g, w1, w2, gf)


def _ab_project(x, gn, win, lng, lnb, d_pool, d_gmlp):
    h = _rmsnorm(x, gn).astype(BF16)
    z = _dot(h, win)
    a = z[:, :d_pool]
    u = _gelu(z[:, d_pool:d_pool + d_gmlp])
    v = _gelu(z[:, d_pool + d_gmlp:])
    return a, u, _layernorm(v, lng, lnb)


def _pool_project(pooled, pw_ref, scale):
    group = pw_ref.shape[1]
    mixed = [_dot(pooled[:, g * group:(g + 1) * group].astype(BF16), pw_ref[g])
             for g in range(pw_ref.shape[0])]
    return jnp.concatenate(mixed, axis=1) * scale


def _ab_prompt_kernel(x_ref, gn_ref, win_ref, pw_ref, ps_ref, lng_ref, lnb_ref, ws_ref, bst_ref, wout_ref,
                      y_ref, hist_ref, aext_ref, tail_ref, *, tm):
    j = pl.program_id(1)
    d_pool = ps_ref.shape[1]
    d_gmlp = lng_ref.shape[1]
    n_heads = ws_ref.shape[0]
    head = d_gmlp // n_heads
    group = d_pool // len(POOL_WINDOWS)
    halo = 2 * SUBLANES

    x = x_ref[...]
    a, u, v = _ab_project(x, gn_ref[...], win_ref[...], lng_ref[...], lnb_ref[...], d_pool, d_gmlp)

    @pl.when(j == 0)
    def _():
        tail_ref[...] = jnp.zeros_like(tail_ref)

    aext_ref[0:halo, :] = tail_ref[...]
    aext_ref[halo:halo + tm, :] = a
    tail_ref[...] = a[tm - halo:, :]

    pos = j * tm + lax.broadcasted_iota(jnp.int32, (tm, 1), 0)
    pooled = []
    for g, w in enumerate(POOL_WINDOWS):
        cols = slice(g * group, (g + 1) * group)
        s = a[:, cols]
        for k in range(1, w):
            s = s + aext_ref[halo - k:halo - k + tm, cols]
        cnt = jnp.minimum(pos + 1, w).astype(F32)
        pooled.append(s / cnt - a[:, cols])
    pool_out = _pool_project(jnp.concatenate(pooled, axis=1), pw_ref, ps_ref[...])

    n_chunks = tm // GMLP_CHUNK
    vb = v.astype(BF16)
    tril = (lax.broadcasted_iota(jnp.int32, (GMLP_CHUNK, GMLP_CHUNK), 0)
            >= lax.broadcasted_iota(jnp.int32, (GMLP_CHUNK, GMLP_CHUNK), 1))
    mixed = []
    for h in range(n_heads):
        wm = jnp.where(tril, ws_ref[h], 0.0).astype(BF16)
        rhs = jnp.concatenate([vb[c * GMLP_CHUNK:(c + 1) * GMLP_CHUNK, h * head:(h + 1) * head]
                               for c in range(n_chunks)], axis=1)
        mixed.append(_dot(wm, rhs) + bst_ref[:, h:h + 1])
    gate = jnp.concatenate(
        [jnp.concatenate([mixed[h][:, c * head:(c + 1) * head] for h in range(n_heads)], axis=1)
         for c in range(n_chunks)], axis=0)

    cat = jnp.concatenate([pool_out, u * gate], axis=1).astype(BF16)
    y_ref[...] = x + _dot(cat, wout_ref[...])

    @pl.when(j == pl.num_programs(1) - 1)
    def _():
        hist_ref[...] = a[tm - POOL_HIST:, :]


def _ab_prompt(x, gn, win, pw, ps, lng, lnb, ws, bst, wout, *, batch, seq, tm):
    d = x.shape[1]
    d_pool = ps.shape[1]
    assert tm % GMLP_CHUNK == 0 and seq % tm == 0
    tiles = seq // tm
    halo = 2 * SUBLANES
    weights = (gn, win, pw, ps, lng, lnb, ws, bst, wout)
    return pl.pallas_call(
        functools.partial(_ab_prompt_kernel, tm=tm),
        grid=(batch, tiles),
        in_specs=[pl.BlockSpec((tm, d), lambda b, j: (b * tiles + j, 0))]
                 + [_resident(w.shape) for w in weights],
        out_specs=[pl.BlockSpec((tm, d), lambda b, j: (b * tiles + j, 0)),
                   pl.BlockSpec((None, POOL_HIST, d_pool), lambda b, j: (b, 0, 0))],
        out_shape=[jax.ShapeDtypeStruct((batch * seq, d), F32),
                   jax.ShapeDtypeStruct((batch, POOL_HIST, d_pool), F32)],
        scratch_shapes=[pltpu.VMEM((halo + tm, d_pool), F32), pltpu.VMEM((halo, d_pool), F32)],
        compiler_params=_params(2),
        name="ab_prompt",
    )(x, *weights)


def _ab_sample_kernel(x_ref, hist_ref, gn_ref, win_ref, pw_ref, ps_ref, lng_ref, lnb_ref, coef_ref, bias_ref,
                      wout_ref, y_ref, nhist_ref, v_ref, *, nb, nt, past_len):
    d_pool = ps_ref.shape[1]
    d_gmlp = lng_ref.shape[1]
    group = d_pool // len(POOL_WINDOWS)

    x = x_ref[...]
    a, u, v = _ab_project(x, gn_ref[...], win_ref[...], lng_ref[...], lnb_ref[...], d_pool, d_gmlp)
    v_ref[...] = v

    def ext(k):
        return hist_ref[k] if k < POOL_HIST else a[(k - POOL_HIST) * nb:(k - POOL_HIST + 1) * nb, :]

    for k in range(POOL_HIST):
        nhist_ref[k] = ext(k + nt)

    pooled, gate = [], []
    for t in range(nt):
        groups = []
        for g, w in enumerate(POOL_WINDOWS):
            cols = slice(g * group, (g + 1) * group)
            s = ext(POOL_HIST + t)[:, cols]
            for k in range(1, w):
                s = s + ext(POOL_HIST + t - k)[:, cols]
            cnt = float(min(past_len + t + 1, w))
            groups.append(s / cnt - a[t * nb:(t + 1) * nb, cols])
        pooled.append(jnp.concatenate(groups, axis=1))
        m = bias_ref[t:t + 1, :]
        for s_ in range(t + 1):
            vs = v[s_ * nb:(s_ + 1) * nb, :].astype(BF16).astype(F32)
            m = m + coef_ref[t * nt + s_:t * nt + s_ + 1, :].astype(BF16).astype(F32) * vs
        gate.append(m)
    pool_out = _pool_project(jnp.concatenate(pooled, axis=0), pw_ref, ps_ref[...])
    cat = jnp.concatenate([pool_out, u * jnp.concatenate(gate, axis=0)], axis=1).astype(BF16)
    y_ref[...] = x + _dot(cat, wout_ref[...])


def _ab_sample(x, hist_t, gn, win, pw, ps, lng, lnb, coef, bias, wout, *, nb, nt, past_len):
    rows, d = x.shape
    d_pool = ps.shape[1]
    d_gmlp = lng.shape[1]
    operands = (x, hist_t, gn, win, pw, ps, lng, lnb, coef, bias, wout)
    return pl.pallas_call(
        functools.partial(_ab_sample_kernel, nb=nb, nt=nt, past_len=past_len),
        grid=(1,),
        in_specs=[_resident(o.shape) for o in operands],
        out_specs=[_resident((rows, d)), _resident((POOL_HIST, nb, d_pool)), _resident((rows, d_gmlp))],
        out_shape=[jax.ShapeDtypeStruct((rows, d), F32),
                   jax.ShapeDtypeStruct((POOL_HIST, nb, d_pool), F32),
                   jax.ShapeDtypeStruct((rows, d_gmlp), F32)],
        compiler_params=_params(1),
        name="ab_sample",
    )(*operands)


def _glu_project(x, gn, wpw1, d_conv):
    h = _rmsnorm(x, gn).astype(BF16)
    z = _dot(h, wpw1)
    return z[:, :d_conv] * jax.nn.sigmoid(z[:, d_conv:])


def _conv_tail(c, lng, lnb, wpw2):
    c = _layernorm(c, lng, lnb)
    return _dot((c * jax.nn.sigmoid(c)).astype(BF16), wpw2)


def _c_prompt_kernel(x_ref, gn_ref, wpw1_ref, wdw_ref, bdw_ref, lng_ref, lnb_ref, wpw2_ref,
                     y_ref, hist_ref, ext_ref, conv_ref, tail_ref, *, tm):
    j = pl.program_id(1)
    d_conv = bdw_ref.shape[1]
    halo = 4 * SUBLANES
    lead = halo - CONV_HIST

    x = x_ref[...]
    glu = _glu_project(x, gn_ref[...], wpw1_ref[...], d_conv)

    @pl.when(j == 0)
    def _():
        tail_ref[...] = jnp.zeros_like(tail_ref)

    ext_ref[0:halo, :] = tail_ref[...]
    ext_ref[halo:halo + tm, :] = glu
    tail_ref[...] = glu[tm - halo:, :]

    rb = CONV_ROW_BLOCK
    for r in range(tm // rb):
        acc = jnp.broadcast_to(bdw_ref[...], (rb, d_conv))
        for k in range(CONV_WIDTH):
            acc = acc + wdw_ref[k:k + 1, :] * ext_ref[lead + r * rb + k:lead + r * rb + k + rb, :]
        conv_ref[r * rb:(r + 1) * rb, :] = acc

    y_ref[...] = x + _conv_tail(conv_ref[...], lng_ref[...], lnb_ref[...], wpw2_ref[...])

    @pl.when(j == pl.num_programs(1) - 1)
    def _():
        hist_ref[...] = glu[tm - CONV_HIST:, :]


def _c_prompt(x, gn, wpw1, wdw, bdw, lng, lnb, wpw2, *, batch, seq, tm):
    d = x.shape[1]
    d_conv = bdw.shape[1]
    assert seq % tm == 0 and tm % CONV_ROW_BLOCK == 0
    tiles = seq // tm
    halo = 4 * SUBLANES
    weights = (gn, wpw1, wdw, bdw, lng, lnb, wpw2)
    return pl.pallas_call(
        functools.partial(_c_prompt_kernel, tm=tm),
        grid=(batch, tiles),
        in_specs=[pl.BlockSpec((tm, d), lambda b, j: (b * tiles + j, 0))]
                 + [_resident(w.shape) for w in weights],
        out_specs=[pl.BlockSpec((tm, d), lambda b, j: (b * tiles + j, 0)),
                   pl.BlockSpec((None, CONV_HIST, d_conv), lambda b, j: (b, 0, 0))],
        out_shape=[jax.ShapeDtypeStruct((batch * seq, d), F32),
                   jax.ShapeDtypeStruct((batch, CONV_HIST, d_conv), F32)],
        scratch_shapes=[pltpu.VMEM((halo + tm, d_conv), F32), pltpu.VMEM((tm, d_conv), F32),
                        pltpu.VMEM((halo, d_conv), F32)],
        compiler_params=_params(2),
        name="c_prompt",
    )(x, *weights)


def _c_sample_kernel(x_ref, hist_ref, gn_ref, wpw1_ref, wdw_ref, bdw_ref, lng_ref, lnb_ref, wpw2_ref,
                     y_ref, nhist_ref, conv_ref, *, nb, nt):
    d_conv = bdw_ref.shape[1]
    x = x_ref[...]
    glu = _glu_project(x, gn_ref[...], wpw1_ref[...], d_conv)

    def ext(k):
        return hist_ref[k] if k < CONV_HIST else glu[(k - CONV_HIST) * nb:(k - CONV_HIST + 1) * nb, :]

    for k in range(CONV_HIST):
        nhist_ref[k] = ext(k + nt)

    for t in range(nt):
        acc = jnp.broadcast_to(bdw_ref[...], (nb, d_conv))
        for k in range(CONV_WIDTH):
            acc = acc + wdw_ref[k:k + 1, :] * ext(t + k)
        conv_ref[t * nb:(t + 1) * nb, :] = acc

    y_ref[...] = x + _conv_tail(conv_ref[...], lng_ref[...], lnb_ref[...], wpw2_ref[...])


def _c_sample(x, hist_t, gn, wpw1, wdw, bdw, lng, lnb, wpw2, *, nb, nt):
    rows, d = x.shape
    d_conv = bdw.shape[1]
    operands = (x, hist_t, gn, wpw1, wdw, bdw, lng, lnb, wpw2)
    return pl.pallas_call(
        functools.partial(_c_sample_kernel, nb=nb, nt=nt),
        grid=(1,),
        in_specs=[_resident(o.shape) for o in operands],
        out_specs=[_resident((rows, d)), _resident((CONV_HIST, nb, d_conv))],
        out_shape=[jax.ShapeDtypeStruct((rows, d), F32),
                   jax.ShapeDtypeStruct((CONV_HIST, nb, d_conv), F32)],
        scratch_shapes=[pltpu.VMEM((rows, d_conv), F32)],
        compiler_params=_params(1),
        name="c_sample",
    )(*operands)


def kernel(x_prompt, x_sample, state_pool, state_conv, norm_mix, norm_ffn, norm_final, ab_w_in, ab_pool_w, ab_pool_scale, ab_ln_g, ab_ln_b, ab_ws, ab_bs, ab_w_out, c_w_pw1, c_w_dw, c_b_dw, c_ln_g, c_ln_b, c_w_pw2, ffn_w1, ffn_w2):
    batch, seq, d = x_prompt.shape
    nb, nt, _ = x_sample.shape
    depth = norm_mix.shape[0]
    past_len = PAST_LEN
    assert past_len % GMLP_CHUNK == 0 and nt <= GMLP_CHUNK

    row = lambda v: v.reshape(1, -1)
    xp = x_prompt.reshape(batch * seq, d)
    xs = jnp.transpose(x_sample, (1, 0, 2)).reshape(nt * nb, d)
    gf = row(norm_final)

    pool_p, pool_s, conv_p, conv_s, gate_v_s = [], [], [], [], []
    for layer in range(depth):
        i = layer // 2
        gn = row(norm_mix[layer])
        if layer % 2 == 0:
            win, wout = ab_w_in[i].astype(BF16), ab_w_out[i].astype(BF16)
            pw = ab_pool_w[i].astype(BF16)
            ps, lng, lnb = row(ab_pool_scale[i]), row(ab_ln_g[i]), row(ab_ln_b[i])
            head = lng.shape[1] // ab_ws.shape[1]
            xp, hp = _ab_prompt(xp, gn, win, pw, ps, lng, lnb, ab_ws[i], ab_bs[i].T, wout,
                                batch=batch, seq=seq, tm=ROW_TILE)
            coef = jnp.repeat(jnp.transpose(ab_ws[i][:, :nt, :nt], (1, 2, 0)).reshape(nt * nt, -1), head, axis=1)
            bias = jnp.repeat(ab_bs[i][:, :nt].T, head, axis=1)
            xs, hs, vs = _ab_sample(xs, jnp.transpose(state_pool[i], (1, 0, 2)), gn, win, pw, ps, lng, lnb,
                                    coef, bias, wout, nb=nb, nt=nt, past_len=past_len)
            pool_p.append(hp)
            pool_s.append(jnp.transpose(hs, (1, 0, 2)))
            gate_v_s.append(jnp.transpose(vs.reshape(nt, nb, -1), (1, 0, 2)))
        else:
            wpw1, wpw2 = c_w_pw1[i].astype(BF16), c_w_pw2[i].astype(BF16)
            bdw, lng, lnb = row(c_b_dw[i]), row(c_ln_g[i]), row(c_ln_b[i])
            xp, hp = _c_prompt(xp, gn, wpw1, c_w_dw[i], bdw, lng, lnb, wpw2, batch=batch, seq=seq, tm=ROW_TILE)
            xs, hs = _c_sample(xs, jnp.transpose(state_conv[i], (1, 0, 2)), gn, wpw1, c_w_dw[i], bdw, lng, lnb,
                               wpw2, nb=nb, nt=nt)
            conv_p.append(hp)
            conv_s.append(jnp.transpose(hs, (1, 0, 2)))
        gff = row(norm_ffn[layer])
        w1, w2 = ffn_w1[layer].astype(BF16), ffn_w2[layer].astype(BF16)
        final = layer == depth - 1
        xp = _ffn(xp, gff, w1, w2, gf, final=final, tm=ROW_TILE)
        xs = _ffn(xs, gff, w1, w2, gf, final=final, tm=nt * nb)

    y_prompt = xp.reshape(batch, seq, d)
    y_sample = jnp.transpose(xs.reshape(nt, nb, d), (1, 0, 2))
    return (y_prompt, y_sample, jnp.stack(pool_p), jnp.stack(pool_s), jnp.stack(conv_p), jnp.stack(conv_s),
            jnp.stack(gate_v_s))
```

```python
import functools

import jax
import jax.numpy as jnp
from jax import lax
from jax.experimental import pallas as pl
from jax.experimental.pallas import tpu as pltpu

EPS = 1e-6
POOL_WINDOWS = (2, 4, 8, 16)
POOL_HIST = max(POOL_WINDOWS) - 1
GMLP_CHUNK = 128
CONV_WIDTH = 31
CONV_HIST = CONV_WIDTH - 1
PAST_LEN = 16384

LANES = 128
SUBLANES = 8
VMEM_LIMIT_BYTES = 56 * 1024 * 1024
ROW_TILE = 512
CONV_ROW_BLOCK = 32
CONV_HALO = 4 * SUBLANES

F32 = jnp.float32
BF16 = jnp.bfloat16


def _rmsnorm(x, g):
    return x * lax.rsqrt(jnp.mean(x * x, axis=-1, keepdims=True) + EPS) * g


def _layernorm(x, g, b):
    mu = jnp.mean(x, axis=-1, keepdims=True)
    xc = x - mu
    var = jnp.mean(xc * xc, axis=-1, keepdims=True)
    return xc * lax.rsqrt(var + EPS) * g + b


def _gelu(x):
    return 0.5 * x * (1.0 + lax.erf(x * (2.0 ** -0.5)))


def _dot(a, b):
    return jnp.dot(a, b, preferred_element_type=F32)


def _resident(shape):
    zeros = (0,) * len(shape)
    return pl.BlockSpec(shape, lambda *_: zeros, pipeline_mode=pl.Buffered(1))


def _params(n_grid_dims):
    return pltpu.CompilerParams(dimension_semantics=("arbitrary",) * n_grid_dims,
                                vmem_limit_bytes=VMEM_LIMIT_BYTES)


def _ffn_kernel(x_ref, g_ref, w1_ref, w2_ref, gf_ref, o_ref, *, final):
    x = x_ref[...]
    h = _rmsnorm(x, g_ref[...]).astype(BF16)
    a = _dot(h, w1_ref[...])
    a = jnp.square(jnp.maximum(a, 0.0)).astype(BF16)
    y = x + _dot(a, w2_ref[...])
    if final:
        y = _rmsnorm(y, gf_ref[...])
    o_ref[...] = y


def _ffn(x, g, w1, w2, gf, *, final, tm):
    rows, d = x.shape
    dff = w1.shape[1]
    return pl.pallas_call(
        functools.partial(_ffn_kernel, final=final),
        grid=(rows // tm,),
        in_specs=[pl.BlockSpec((tm, d), lambda i: (i, 0)),
                  _resident((1, d)), _resident((d, dff)), _resident((dff, d)), _resident((1, d))],
        out_specs=pl.BlockSpec((tm, d), lambda i: (i, 0)),
        out_shape=jax.ShapeDtypeStruct((rows, d), F32),
        compiler_params=_params(1),
        name="ffn",
    )(x, g, w1, w2, gf)


def _ab_project(x, gn, win, lng, lnb, d_pool, d_gmlp):
    h = _rmsnorm(x, gn).astype(BF16)
    z = _dot(h, win)
    a = z[:, :d_pool]
    u = _gelu(z[:, d_pool:d_pool + d_gmlp])
    v = _gelu(z[:, d_pool + d_gmlp:])
    return a, u, _layernorm(v, lng, lnb)


def _pool_project(pooled, pw_ref, scale):
    group = pw_ref.shape[1]
    mixed = [_dot(pooled[:, g * group:(g + 1) * group].astype(BF16), pw_ref[g])
             for g in range(pw_ref.shape[0])]
    return jnp.concatenate(mixed, axis=1) * scale


def _ab_prompt_kernel(x_ref, gn_ref, win_ref, pw_ref, ps_ref, lng_ref, lnb_ref, ws_ref, bst_ref, wout_ref,
                      y_ref, hist_ref, aext_ref, tail_ref, *, tm):
    j = pl.program_id(1)
    d_pool = ps_ref.shape[1]
    d_gmlp = lng_ref.shape[1]
    n_heads = ws_ref.shape[0]
    head = d_gmlp // n_heads
    group = d_pool // len(POOL_WINDOWS)
    halo = 2 * SUBLANES

    x = x_ref[...]
    a, u, v = _ab_project(x, gn_ref[...], win_ref[...], lng_ref[...], lnb_ref[...], d_pool, d_gmlp)

    @pl.when(j == 0)
    def _():
        tail_ref[...] = jnp.zeros_like(tail_ref)

    aext_ref[0:halo, :] = tail_ref[...]
    aext_ref[halo:halo + tm, :] = a
    tail_ref[...] = a[tm - halo:, :]

    pos = j * tm + lax.broadcasted_iota(jnp.int32, (tm, 1), 0)
    pooled = []
    for g, w in enumerate(POOL_WINDOWS):
        cols = slice(g * group, (g + 1) * group)
        s = a[:, cols]
        for k in range(1, w):
            s = s + aext_ref[halo - k:halo - k + tm, cols]
        cnt = jnp.minimum(pos + 1, w).astype(F32)
        pooled.append(s / cnt - a[:, cols])
    pool_out = _pool_project(jnp.concatenate(pooled, axis=1), pw_ref, ps_ref[...])

    n_chunks = tm // GMLP_CHUNK
    vb = v.astype(BF16)
    tril = (lax.broadcasted_iota(jnp.int32, (GMLP_CHUNK, GMLP_CHUNK), 0)
            >= lax.broadcasted_iota(jnp.int32, (GMLP_CHUNK, GMLP_CHUNK), 1))
    mixed = []
    for h in range(n_heads):
        wm = jnp.where(tril, ws_ref[h], 0.0).astype(BF16)
        rhs = jnp.concatenate([vb[c * GMLP_CHUNK:(c + 1) * GMLP_CHUNK, h * head:(h + 1) * head]
                               for c in range(n_chunks)], axis=1)
        mixed.append(_dot(wm, rhs) + bst_ref[:, h:h + 1])
    gate = jnp.concatenate(
        [jnp.concatenate([mixed[h][:, c * head:(c + 1) * head] for h in range(n_heads)], axis=1)
         for c in range(n_chunks)], axis=0)

    cat = jnp.concatenate([pool_out, u * gate], axis=1).astype(BF16)
    y_ref[...] = x + _dot(cat, wout_ref[...])

    @pl.when(j == pl.num_programs(1) - 1)
    def _():
        hist_ref[...] = a[tm - POOL_HIST:, :]


def _ab_prompt(x, gn, win, pw, ps, lng, lnb, ws, bst, wout, *, batch, seq, tm):
    d = x.shape[1]
    d_pool = ps.shape[1]
    assert tm % GMLP_CHUNK == 0 and seq % tm == 0
    tiles = seq // tm
    halo = 2 * SUBLANES
    weights = (gn, win, pw, ps, lng, lnb, ws, bst, wout)
    return pl.pallas_call(
        functools.partial(_ab_prompt_kernel, tm=tm),
        grid=(batch, tiles),
        in_specs=[pl.BlockSpec((tm, d), lambda b, j: (b * tiles + j, 0))]
                 + [_resident(w.shape) for w in weights],
        out_specs=[pl.BlockSpec((tm, d), lambda b, j: (b * tiles + j, 0)),
                   pl.BlockSpec((None, POOL_HIST, d_pool), lambda b, j: (b, 0, 0))],
        out_shape=[jax.ShapeDtypeStruct((batch * seq, d), F32),
                   jax.ShapeDtypeStruct((batch, POOL_HIST, d_pool), F32)],
        scratch_shapes=[pltpu.VMEM((halo + tm, d_pool), F32), pltpu.VMEM((halo, d_pool), F32)],
        compiler_params=_params(2),
        name="ab_prompt",
    )(x, *weights)


def _ab_sample_kernel(x_ref, hist_ref, gn_ref, win_ref, pw_ref, ps_ref, lng_ref, lnb_ref, coef_ref, bias_ref,
                      wout_ref, y_ref, nhist_ref, v_ref, *, nb, nt, past_len):
    d_pool = ps_ref.shape[1]
    d_gmlp = lng_ref.shape[1]
    group = d_pool // len(POOL_WINDOWS)

    x = x_ref[...]
    a, u, v = _ab_project(x, gn_ref[...], win_ref[...], lng_ref[...], lnb_ref[...], d_pool, d_gmlp)
    v_ref[...] = v

    def ext(k):
        return hist_ref[k] if k < POOL_HIST else a[(k - POOL_HIST) * nb:(k - POOL_HIST + 1) * nb, :]

    for k in range(POOL_HIST):
        nhist_ref[k] = ext(k + nt)

    pooled, gate = [], []
    for t in range(nt):
        groups = []
        for g, w in enumerate(POOL_WINDOWS):
            cols = slice(g * group, (g + 1) * group)
            s = ext(POOL_HIST + t)[:, cols]
            for k in range(1, w):
                s = s + ext(POOL_HIST + t - k)[:, cols]
            cnt = float(min(past_len + t + 1, w))
            groups.append(s / cnt - a[t * nb:(t + 1) * nb, cols])
        pooled.append(jnp.concatenate(groups, axis=1))
        m = bias_ref[t:t + 1, :]
        for s_ in range(t + 1):
            vs = v[s_ * nb:(s_ + 1) * nb, :].astype(BF16).astype(F32)
            m = m + coef_ref[t * nt + s_:t * nt + s_ + 1, :].astype(BF16).astype(F32) * vs
        gate.append(m)
    pool_out = _pool_project(jnp.concatenate(pooled, axis=0), pw_ref, ps_ref[...])
    cat = jnp.concatenate([pool_out, u * jnp.concatenate(gate, axis=0)], axis=1).astype(BF16)
    y_ref[...] = x + _dot(cat, wout_ref[...])


def _ab_sample(x, hist_t, gn, win, pw, ps, lng, lnb, coef, bias, wout, *, nb, nt, past_len):
    rows, d = x.shape
    d_pool = ps.shape[1]
    d_gmlp = lng.shape[1]
    operands = (x, hist_t, gn, win, pw, ps, lng, lnb, coef, bias, wout)
    return pl.pallas_call(
        functools.partial(_ab_sample_kernel, nb=nb, nt=nt, past_len=past_len),
        grid=(1,),
        in_specs=[_resident(o.shape) for o in operands],
        out_specs=[_resident((rows, d)), _resident((POOL_HIST, nb, d_pool)), _resident((rows, d_gmlp))],
        out_shape=[jax.ShapeDtypeStruct((rows, d), F32),
                   jax.ShapeDtypeStruct((POOL_HIST, nb, d_pool), F32),
                   jax.ShapeDtypeStruct((rows, d_gmlp), F32)],
        compiler_params=_params(1),
        name="ab_sample",
    )(*operands)


def _glu_project(x, gn, wpw1, d_conv):
    h = _rmsnorm(x, gn).astype(BF16)
    z = _dot(h, wpw1)
    return z[:, :d_conv] * jax.nn.sigmoid(z[:, d_conv:])


def _conv_tail(c, lng, lnb, wpw2):
    c = _layernorm(c, lng, lnb)
    return _dot((c * jax.nn.sigmoid(c)).astype(BF16), wpw2)


def _c_prompt_kernel(x_ref, gn_ref, wpw1_ref, wdw_ref, bdw_ref, lng_ref, lnb_ref, wpw2_ref,
                     y_ref, hist_ref, ext_ref, conv_ref, *, tm):
    j = pl.program_id(1)
    d_conv = bdw_ref.shape[1]
    halo = CONV_HALO
    lead = halo - CONV_HIST
    n_pairs = d_conv // (2 * LANES)

    x = x_ref[...]
    glu = _glu_project(x, gn_ref[...], wpw1_ref[...], d_conv)

    @pl.when(j == 0)
    def _():
        ext_ref[:, 0:2 * halo, :] = jnp.zeros((n_pairs, 2 * halo, LANES), F32)

    @pl.when(j > 0)
    def _():
        ext_ref[:, 0:2 * halo, :] = ext_ref[:, 2 * tm:2 * (tm + halo), :]

    for cb in range(d_conv // LANES):
        ext_ref[cb // 2, pl.ds(2 * halo + cb % 2, tm, stride=2), :] = glu[:, cb * LANES:(cb + 1) * LANES]

    rb = CONV_ROW_BLOCK
    for cb in range(d_conv // LANES):
        cols = slice(cb * LANES, (cb + 1) * LANES)
        w = [jnp.broadcast_to(wdw_ref[k:k + 1, cols], (rb, LANES)) for k in range(CONV_WIDTH)]
        b = jnp.broadcast_to(bdw_ref[:, cols], (rb, LANES))
        for r in range(0, tm, rb):
            acc = b
            for k in range(CONV_WIDTH):
                acc = acc + w[k] * ext_ref[cb // 2, pl.ds(2 * (lead + r + k) + cb % 2, rb, stride=2), :]
            conv_ref[r:r + rb, cols] = acc

    y_ref[...] = x + _conv_tail(conv_ref[...], lng_ref[...], lnb_ref[...], wpw2_ref[...])

    @pl.when(j == pl.num_programs(1) - 1)
    def _():
        hist_ref[...] = glu[tm - CONV_HIST:, :]


def _c_prompt(x, gn, wpw1, wdw, bdw, lng, lnb, wpw2, *, batch, seq, tm):
    d = x.shape[1]
    d_conv = bdw.shape[1]
    assert seq % tm == 0 and tm % CONV_ROW_BLOCK == 0 and d_conv % (2 * LANES) == 0
    tiles = seq // tm
    weights = (gn, wpw1, wdw, bdw, lng, lnb, wpw2)
    return pl.pallas_call(
        functools.partial(_c_prompt_kernel, tm=tm),
        grid=(batch, tiles),
        in_specs=[pl.BlockSpec((tm, d), lambda b, j: (b * tiles + j, 0))]
                 + [_resident(w.shape) for w in weights],
        out_specs=[pl.BlockSpec((tm, d), lambda b, j: (b * tiles + j, 0)),
                   pl.BlockSpec((None, CONV_HIST, d_conv), lambda b, j: (b, 0, 0))],
        out_shape=[jax.ShapeDtypeStruct((batch * seq, d), F32),
                   jax.ShapeDtypeStruct((batch, CONV_HIST, d_conv), F32)],
        scratch_shapes=[pltpu.VMEM((d_conv // (2 * LANES), 2 * (CONV_HALO + tm), LANES), F32),
                        pltpu.VMEM((tm, d_conv), F32)],
        compiler_params=_params(2),
        name="c_prompt",
    )(x, *weights)


def _c_sample_kernel(x_ref, hist_ref, gn_ref, wpw1_ref, wdw_ref, bdw_ref, lng_ref, lnb_ref, wpw2_ref,
                     y_ref, nhist_ref, conv_ref, *, nb, nt):
    d_conv = bdw_ref.shape[1]
    x = x_ref[...]
    glu = _glu_project(x, gn_ref[...], wpw1_ref[...], d_conv)

    def ext(k):
        return hist_ref[k] if k < CONV_HIST else glu[(k - CONV_HIST) * nb:(k - CONV_HIST + 1) * nb, :]

    for k in range(CONV_HIST):
        nhist_ref[k] = ext(k + nt)

    for t in range(nt):
        acc = jnp.broadcast_to(bdw_ref[...], (nb, d_conv))
        for k in range(CONV_WIDTH):
            acc = acc + wdw_ref[k:k + 1, :] * ext(t + k)
        conv_ref[t * nb:(t + 1) * nb, :] = acc

    y_ref[...] = x + _conv_tail(conv_ref[...], lng_ref[...], lnb_ref[...], wpw2_ref[...])


def _c_sample(x, hist_t, gn, wpw1, wdw, bdw, lng, lnb, wpw2, *, nb, nt):
    rows, d = x.shape
    d_conv = bdw.shape[1]
    operands = (x, hist_t, gn, wpw1, wdw, bdw, lng, lnb, wpw2)
    return pl.pallas_call(
        functools.partial(_c_sample_kernel, nb=nb, nt=nt),
        grid=(1,),
        in_specs=[_resident(o.shape) for o in operands],
        out_specs=[_resident((rows, d)), _resident((CONV_HIST, nb, d_conv))],
        out_shape=[jax.ShapeDtypeStruct((rows, d), F32),
                   jax.ShapeDtypeStruct((CONV_HIST, nb, d_conv), F32)],
        scratch_shapes=[pltpu.VMEM((rows, d_conv), F32)],
        compiler_params=_params(1),
        name="c_sample",
    )(*operands)


def kernel(x_prompt, x_sample, state_pool, state_conv, norm_mix, norm_ffn, norm_final, ab_w_in, ab_pool_w, ab_pool_scale, ab_ln_g, ab_ln_b, ab_ws, ab_bs, ab_w_out, c_w_pw1, c_w_dw, c_b_dw, c_ln_g, c_ln_b, c_w_pw2, ffn_w1, ffn_w2):
    batch, seq, d = x_prompt.shape
    nb, nt, _ = x_sample.shape
    depth = norm_mix.shape[0]
    past_len = PAST_LEN
    assert past_len % GMLP_CHUNK == 0 and nt <= GMLP_CHUNK

    row = lambda v: v.reshape(1, -1)
    xp = x_prompt.reshape(batch * seq, d)
    xs = jnp.transpose(x_sample, (1, 0, 2)).reshape(nt * nb, d)
    gf = row(norm_final)

    pool_p, pool_s, conv_p, conv_s, gate_v_s = [], [], [], [], []
    for layer in range(depth):
        i = layer // 2
        gn = row(norm_mix[layer])
        if layer % 2 == 0:
            win, wout = ab_w_in[i].astype(BF16), ab_w_out[i].astype(BF16)
            pw = ab_pool_w[i].astype(BF16)
            ps, lng, lnb = row(ab_pool_scale[i]), row(ab_ln_g[i]), row(ab_ln_b[i])
            head = lng.shape[1] // ab_ws.shape[1]
            xp, hp = _ab_prompt(xp, gn, win, pw, ps, lng, lnb, ab_ws[i], ab_bs[i].T, wout,
                                batch=batch, seq=seq, tm=ROW_TILE)
            coef = jnp.repeat(jnp.transpose(ab_ws[i][:, :nt, :nt], (1, 2, 0)).reshape(nt * nt, -1), head, axis=1)
            bias = jnp.repeat(ab_bs[i][:, :nt].T, head, axis=1)
            xs, hs, vs = _ab_sample(xs, jnp.transpose(state_pool[i], (1, 0, 2)), gn, win, pw, ps, lng, lnb,
                                    coef, bias, wout, nb=nb, nt=nt, past_len=past_len)
            pool_p.append(hp)
            pool_s.append(jnp.transpose(hs, (1, 0, 2)))
            gate_v_s.append(jnp.transpose(vs.reshape(nt, nb, -1), (1, 0, 2)))
        else:
            wpw1, wpw2 = c_w_pw1[i].astype(BF16), c_w_pw2[i].astype(BF16)
            bdw, lng, lnb = row(c_b_dw[i]), row(c_ln_g[i]), row(c_ln_b[i])
            xp, hp = _c_prompt(xp, gn, wpw1, c_w_dw[i], bdw, lng, lnb, wpw2, batch=batch, seq=seq, tm=ROW_TILE)
            xs, hs = _c_sample(xs, jnp.transpose(state_conv[i], (1, 0, 2)), gn, wpw1, c_w_dw[i], bdw, lng, lnb,
                               wpw2, nb=nb, nt=nt)
            conv_p.append(hp)
            conv_s.append(jnp.transpose(hs, (1, 0, 2)))
        gff = row(norm_ffn[layer])
        w1, w2 = ffn_w1[layer].astype(BF16), ffn_w2[layer].astype(BF16)
        final = layer == depth - 1
        xp = _ffn(xp, gff, w1, w2, gf, final=final, tm=ROW_TILE)
        xs = _ffn(xs, gff, w1, w2, gf, final=final, tm=nt * nb)

    y_prompt = xp.reshape(batch, seq, d)
    y_sample = jnp.transpose(xs.reshape(nt, nb, d), (1, 0, 2))
    return (y_prompt, y_sample, jnp.stack(pool_p), jnp.stack(pool_s), jnp.stack(conv_p), jnp.stack(conv_s),
            jnp.stack(gate_v_s))
```

```python
import functools

import jax
import jax.numpy as jnp
from jax import lax
from jax.experimental import pallas as pl
from jax.experimental.pallas import tpu as pltpu

EPS = 1e-6
POOL_WINDOWS = (2, 4, 8, 16)
POOL_HIST = max(POOL_WINDOWS) - 1
GMLP_CHUNK = 128
CONV_WIDTH = 31
CONV_HIST = CONV_WIDTH - 1
PAST_LEN = 16384

LANES = 128
SUBLANES = 8
VMEM_LIMIT_BYTES = 56 * 1024 * 1024
ROW_TILE = 512
CONV_ROW_BLOCK = 32
CONV_HALO = 4 * SUBLANES

F32 = jnp.float32
BF16 = jnp.bfloat16


def _rmsnorm(x, g):
    return x * lax.rsqrt(jnp.mean(x * x, axis=-1, keepdims=True) + EPS) * g


def _layernorm(x, g, b):
    mu = jnp.mean(x, axis=-1, keepdims=True)
    xc = x - mu
    var = jnp.mean(xc * xc, axis=-1, keepdims=True)
    return xc * lax.rsqrt(var + EPS) * g + b


def _gelu(x):
    return 0.5 * x * (1.0 + lax.erf(x * (2.0 ** -0.5)))


def _dot(a, b):
    return jnp.dot(a, b, preferred_element_type=F32)


def _resident(shape):
    zeros = (0,) * len(shape)
    return pl.BlockSpec(shape, lambda *_: zeros, pipeline_mode=pl.Buffered(1))


def _layer(stacked, layer):
    zeros = (0,) * (stacked.ndim - 1)
    return pl.BlockSpec((None,) + stacked.shape[1:], lambda *_: (layer,) + zeros, pipeline_mode=pl.Buffered(1))


def _params(n_grid_dims):
    return pltpu.CompilerParams(dimension_semantics=("arbitrary",) * n_grid_dims,
                                vmem_limit_bytes=VMEM_LIMIT_BYTES)


def _ffn_kernel(x_ref, g_ref, w1_ref, w2_ref, gf_ref, o_ref, *, final):
    x = x_ref[...]
    h = _rmsnorm(x, g_ref[...])
    a = _dot(h, w1_ref[...])
    a = jnp.square(jnp.maximum(a, 0.0))
    y = x + _dot(a, w2_ref[...])
    if final:
        y = _rmsnorm(y, gf_ref[...])
    o_ref[...] = y


def _ffn(x, g, w1, w2, gf, *, layer, final, tm):
    rows, d = x.shape
    return pl.pallas_call(
        functools.partial(_ffn_kernel, final=final),
        grid=(rows // tm,),
        in_specs=[pl.BlockSpec((tm, d), lambda i: (i, 0)),
                  _resident((1, d)), _layer(w1, layer), _layer(w2, layer), _resident((1, d))],
        out_specs=pl.BlockSpec((tm, d), lambda i: (i, 0)),
        out_shape=jax.ShapeDtypeStruct((rows, d), F32),
        compiler_params=_params(1),
        name="ffn",
    )(x, g, w1, w2, gf)


def _ab_project(x, gn, win, lng, lnb, d_pool, d_gmlp):
    h = _rmsnorm(x, gn)
    z = _dot(h, win)
    a = z[:, :d_pool]
    u = _gelu(z[:, d_pool:d_pool + d_gmlp])
    v = _gelu(z[:, d_pool + d_gmlp:])
    return a, u, _layernorm(v, lng, lnb)


def _pool_project(pooled, pw_ref, scale):
    group = pw_ref.shape[1]
    mixed = [_dot(pooled[:, g * group:(g + 1) * group], pw_ref[g])
             for g in range(pw_ref.shape[0])]
    return jnp.concatenate(mixed, axis=1) * scale


def _ab_prompt_kernel(x_ref, gn_ref, win_ref, pw_ref, ps_ref, lng_ref, lnb_ref, ws_ref, bst_ref, wout_ref,
                      y_ref, hist_ref, aext_ref, tail_ref, *, tm):
    j = pl.program_id(1)
    d_pool = ps_ref.shape[1]
    d_gmlp = lng_ref.shape[1]
    n_heads = ws_ref.shape[0]
    head = d_gmlp // n_heads
    group = d_pool // len(POOL_WINDOWS)
    halo = 2 * SUBLANES

    x = x_ref[...]
    a, u, v = _ab_project(x, gn_ref[...], win_ref[...], lng_ref[...], lnb_ref[...], d_pool, d_gmlp)

    @pl.when(j == 0)
    def _():
        tail_ref[...] = jnp.zeros_like(tail_ref)

    aext_ref[0:halo, :] = tail_ref[...]
    aext_ref[halo:halo + tm, :] = a
    tail_ref[...] = a[tm - halo:, :]

    pos = j * tm + lax.broadcasted_iota(jnp.int32, (tm, 1), 0)
    pooled = []
    for g, w in enumerate(POOL_WINDOWS):
        cols = slice(g * group, (g + 1) * group)
        s = a[:, cols]
        for k in range(1, w):
            s = s + aext_ref[halo - k:halo - k + tm, cols]
        cnt = jnp.minimum(pos + 1, w).astype(F32)
        pooled.append(s / cnt - a[:, cols])
    pool_out = _pool_project(jnp.concatenate(pooled, axis=1), pw_ref, ps_ref[...])

    n_chunks = tm // GMLP_CHUNK
    tril = (lax.broadcasted_iota(jnp.int32, (GMLP_CHUNK, GMLP_CHUNK), 0)
            >= lax.broadcasted_iota(jnp.int32, (GMLP_CHUNK, GMLP_CHUNK), 1))
    mixed = []
    for h in range(n_heads):
        wm = jnp.where(tril, ws_ref[h], 0.0)
        rhs = jnp.concatenate([v[c * GMLP_CHUNK:(c + 1) * GMLP_CHUNK, h * head:(h + 1) * head]
                               for c in range(n_chunks)], axis=1)
        mixed.append(_dot(wm, rhs) + bst_ref[:, h:h + 1])
    gate = jnp.concatenate(
        [jnp.concatenate([mixed[h][:, c * head:(c + 1) * head] for h in range(n_heads)], axis=1)
         for c in range(n_chunks)], axis=0)

    cat = jnp.concatenate([pool_out, u * gate], axis=1)
    y_ref[...] = x + _dot(cat, wout_ref[...])

    @pl.when(j == pl.num_programs(1) - 1)
    def _():
        hist_ref[...] = a[tm - POOL_HIST:, :]


def _ab_prompt(x, gn, win, pw, ps, lng, lnb, ws, bst, wout, *, layer, batch, seq, tm):
    d = x.shape[1]
    d_pool = ps.shape[1]
    assert tm % GMLP_CHUNK == 0 and seq % tm == 0
    tiles = seq // tm
    halo = 2 * SUBLANES
    weights = (gn, win, pw, ps, lng, lnb, ws, bst, wout)
    stacked = (win, pw, ws, wout)
    return pl.pallas_call(
        functools.partial(_ab_prompt_kernel, tm=tm),
        grid=(batch, tiles),
        in_specs=[pl.BlockSpec((tm, d), lambda b, j: (b * tiles + j, 0))]
                 + [_layer(w, layer) if any(w is s for s in stacked) else _resident(w.shape) for w in weights],
        out_specs=[pl.BlockSpec((tm, d), lambda b, j: (b * tiles + j, 0)),
                   pl.BlockSpec((None, POOL_HIST, d_pool), lambda b, j: (b, 0, 0))],
        out_shape=[jax.ShapeDtypeStruct((batch * seq, d), F32),
                   jax.ShapeDtypeStruct((batch, POOL_HIST, d_pool), F32)],
        scratch_shapes=[pltpu.VMEM((halo + tm, d_pool), F32), pltpu.VMEM((halo, d_pool), F32)],
        compiler_params=_params(2),
        name="ab_prompt",
    )(x, *weights)


def _ab_sample_kernel(x_ref, hist_ref, gn_ref, win_ref, pw_ref, ps_ref, lng_ref, lnb_ref, coef_ref, bias_ref,
                      wout_ref, y_ref, nhist_ref, v_ref, *, nb, nt, past_len):
    d_pool = ps_ref.shape[1]
    d_gmlp = lng_ref.shape[1]
    group = d_pool // len(POOL_WINDOWS)

    x = x_ref[...]
    a, u, v = _ab_project(x, gn_ref[...], win_ref[...], lng_ref[...], lnb_ref[...], d_pool, d_gmlp)
    v_ref[...] = v

    def ext(k):
        return hist_ref[k] if k < POOL_HIST else a[(k - POOL_HIST) * nb:(k - POOL_HIST + 1) * nb, :]

    for k in range(POOL_HIST):
        nhist_ref[k] = ext(k + nt)

    pooled, gate = [], []
    for t in range(nt):
        groups = []
        for g, w in enumerate(POOL_WINDOWS):
            cols = slice(g * group, (g + 1) * group)
            s = ext(POOL_HIST + t)[:, cols]
            for k in range(1, w):
                s = s + ext(POOL_HIST + t - k)[:, cols]
            cnt = float(min(past_len + t + 1, w))
            groups.append(s / cnt - a[t * nb:(t + 1) * nb, cols])
        pooled.append(jnp.concatenate(groups, axis=1))
        m = bias_ref[t:t + 1, :]
        for s_ in range(t + 1):
            vs = v[s_ * nb:(s_ + 1) * nb, :].astype(BF16).astype(F32)
            m = m + coef_ref[t * nt + s_:t * nt + s_ + 1, :].astype(BF16).astype(F32) * vs
        gate.append(m)
    pool_out = _pool_project(jnp.concatenate(pooled, axis=0), pw_ref, ps_ref[...])
    cat = jnp.concatenate([pool_out, u * jnp.concatenate(gate, axis=0)], axis=1)
    y_ref[...] = x + _dot(cat, wout_ref[...])


def _ab_sample(x, hist_t, gn, win, pw, ps, lng, lnb, coef, bias, wout, *, layer, nb, nt, past_len):
    rows, d = x.shape
    d_pool = ps.shape[1]
    d_gmlp = lng.shape[1]
    operands = (x, hist_t, gn, win, pw, ps, lng, lnb, coef, bias, wout)
    stacked = (hist_t, win, pw, wout)
    return pl.pallas_call(
        functools.partial(_ab_sample_kernel, nb=nb, nt=nt, past_len=past_len),
        grid=(1,),
        in_specs=[_layer(o, layer) if any(o is s for s in stacked) else _resident(o.shape) for o in operands],
        out_specs=[_resident((rows, d)), _resident((POOL_HIST, nb, d_pool)), _resident((rows, d_gmlp))],
        out_shape=[jax.ShapeDtypeStruct((rows, d), F32),
                   jax.ShapeDtypeStruct((POOL_HIST, nb, d_pool), F32),
                   jax.ShapeDtypeStruct((rows, d_gmlp), F32)],
        compiler_params=_params(1),
        name="ab_sample",
    )(*operands)


def _glu_project(x, gn, wpw1, d_conv):
    h = _rmsnorm(x, gn)
    z = _dot(h, wpw1)
    return z[:, :d_conv] * jax.nn.sigmoid(z[:, d_conv:])


def _conv_tail(c, lng, lnb, wpw2):
    c = _layernorm(c, lng, lnb)
    return _dot(c * jax.nn.sigmoid(c), wpw2)


def _c_prompt_kernel(x_ref, gn_ref, wpw1_ref, wdw_ref, bdw_ref, lng_ref, lnb_ref, wpw2_ref,
                     y_ref, hist_ref, ext_ref, conv_ref, *, tm):
    j = pl.program_id(1)
    d_conv = bdw_ref.shape[1]
    halo = CONV_HALO
    lead = halo - CONV_HIST
    n_pairs = d_conv // (2 * LANES)

    x = x_ref[...]
    glu = _glu_project(x, gn_ref[...], wpw1_ref[...], d_conv)

    @pl.when(j == 0)
    def _():
        ext_ref[:, 0:2 * halo, :] = jnp.zeros((n_pairs, 2 * halo, LANES), F32)

    @pl.when(j > 0)
    def _():
        ext_ref[:, 0:2 * halo, :] = ext_ref[:, 2 * tm:2 * (tm + halo), :]

    for cb in range(d_conv // LANES):
        ext_ref[cb // 2, pl.ds(2 * halo + cb % 2, tm, stride=2), :] = glu[:, cb * LANES:(cb + 1) * LANES]

    rb = CONV_ROW_BLOCK
    for cb in range(d_conv // LANES):
        cols = slice(cb * LANES, (cb + 1) * LANES)
        w = [jnp.broadcast_to(wdw_ref[k:k + 1, cols], (rb, LANES)) for k in range(CONV_WIDTH)]
        b = jnp.broadcast_to(bdw_ref[:, cols], (rb, LANES))
        for r in range(0, tm, rb):
            acc = b
            for k in range(CONV_WIDTH):
                acc = acc + w[k] * ext_ref[cb // 2, pl.ds(2 * (lead + r + k) + cb % 2, rb, stride=2), :]
            conv_ref[r:r + rb, cols] = acc

    y_ref[...] = x + _conv_tail(conv_ref[...], lng_ref[...], lnb_ref[...], wpw2_ref[...])

    @pl.when(j == pl.num_programs(1) - 1)
    def _():
        hist_ref[...] = glu[tm - CONV_HIST:, :]


def _c_prompt(x, gn, wpw1, wdw, bdw, lng, lnb, wpw2, *, layer, batch, seq, tm):
    d = x.shape[1]
    d_conv = bdw.shape[1]
    assert seq % tm == 0 and tm % CONV_ROW_BLOCK == 0 and d_conv % (2 * LANES) == 0
    tiles = seq // tm
    weights = (gn, wpw1, wdw, bdw, lng, lnb, wpw2)
    stacked = (wpw1, wpw2)
    return pl.pallas_call(
        functools.partial(_c_prompt_kernel, tm=tm),
        grid=(batch, tiles),
        in_specs=[pl.BlockSpec((tm, d), lambda b, j: (b * tiles + j, 0))]
                 + [_layer(w, layer) if any(w is s for s in stacked) else _resident(w.shape) for w in weights],
        out_specs=[pl.BlockSpec((tm, d), lambda b, j: (b * tiles + j, 0)),
                   pl.BlockSpec((None, CONV_HIST, d_conv), lambda b, j: (b, 0, 0))],
        out_shape=[jax.ShapeDtypeStruct((batch * seq, d), F32),
                   jax.ShapeDtypeStruct((batch, CONV_HIST, d_conv), F32)],
        scratch_shapes=[pltpu.VMEM((d_conv // (2 * LANES), 2 * (CONV_HALO + tm), LANES), F32),
                        pltpu.VMEM((tm, d_conv), F32)],
        compiler_params=_params(2),
        name="c_prompt",
    )(x, *weights)


def _c_sample_kernel(x_ref, hist_ref, gn_ref, wpw1_ref, wdw_ref, bdw_ref, lng_ref, lnb_ref, wpw2_ref,
                     y_ref, nhist_ref, conv_ref, *, nb, nt):
    d_conv = bdw_ref.shape[1]
    x = x_ref[...]
    glu = _glu_project(x, gn_ref[...], wpw1_ref[...], d_conv)

    def ext(k):
        return hist_ref[k] if k < CONV_HIST else glu[(k - CONV_HIST) * nb:(k - CONV_HIST + 1) * nb, :]

    for k in range(CONV_HIST):
        nhist_ref[k] = ext(k + nt)

    for t in range(nt):
        acc = jnp.broadcast_to(bdw_ref[...], (nb, d_conv))
        for k in range(CONV_WIDTH):
            acc = acc + wdw_ref[k:k + 1, :] * ext(t + k)
        conv_ref[t * nb:(t + 1) * nb, :] = acc

    y_ref[...] = x + _conv_tail(conv_ref[...], lng_ref[...], lnb_ref[...], wpw2_ref[...])


def _c_sample(x, hist_t, gn, wpw1, wdw, bdw, lng, lnb, wpw2, *, layer, nb, nt):
    rows, d = x.shape
    d_conv = bdw.shape[1]
    operands = (x, hist_t, gn, wpw1, wdw, bdw, lng, lnb, wpw2)
    stacked = (hist_t, wpw1, wpw2)
    return pl.pallas_call(
        functools.partial(_c_sample_kernel, nb=nb, nt=nt),
        grid=(1,),
        in_specs=[_layer(o, layer) if any(o is s for s in stacked) else _resident(o.shape) for o in operands],
        out_specs=[_resident((rows, d)), _resident((CONV_HIST, nb, d_conv))],
        out_shape=[jax.ShapeDtypeStruct((rows, d), F32),
                   jax.ShapeDtypeStruct((CONV_HIST, nb, d_conv), F32)],
        scratch_shapes=[pltpu.VMEM((rows, d_conv), F32)],
        compiler_params=_params(1),
        name="c_sample",
    )(*operands)


def kernel(x_prompt, x_sample, state_pool, state_conv, norm_mix, norm_ffn, norm_final, ab_w_in, ab_pool_w, ab_pool_scale, ab_ln_g, ab_ln_b, ab_ws, ab_bs, ab_w_out, c_w_pw1, c_w_dw, c_b_dw, c_ln_g, c_ln_b, c_w_pw2, ffn_w1, ffn_w2):
    batch, seq, d = x_prompt.shape
    nb, nt, _ = x_sample.shape
    depth = norm_mix.shape[0]
    past_len = PAST_LEN
    assert past_len % GMLP_CHUNK == 0 and nt <= GMLP_CHUNK

    row = lambda v: v.reshape(1, -1)
    xp = x_prompt.reshape(batch * seq, d)
    xs = jnp.transpose(x_sample, (1, 0, 2)).reshape(nt * nb, d)
    gf = row(norm_final)
    pool_t = jnp.transpose(state_pool, (0, 2, 1, 3))
    conv_t = jnp.transpose(state_conv, (0, 2, 1, 3))

    pool_p, pool_s, conv_p, conv_s, gate_v_s = [], [], [], [], []
    for layer in range(depth):
        i = layer // 2
        gn = row(norm_mix[layer])
        if layer % 2 == 0:
            ps, lng, lnb = row(ab_pool_scale[i]), row(ab_ln_g[i]), row(ab_ln_b[i])
            head = lng.shape[1] // ab_ws.shape[1]
            xp, hp = _ab_prompt(xp, gn, ab_w_in, ab_pool_w, ps, lng, lnb, ab_ws, ab_bs[i].T, ab_w_out,
                                layer=i, batch=batch, seq=seq, tm=ROW_TILE)
            coef = jnp.repeat(jnp.transpose(ab_ws[i][:, :nt, :nt], (1, 2, 0)).reshape(nt * nt, -1), head, axis=1)
            bias = jnp.repeat(ab_bs[i][:, :nt].T, head, axis=1)
            xs, hs, vs = _ab_sample(xs, pool_t, gn, ab_w_in, ab_pool_w, ps, lng, lnb, coef, bias, ab_w_out,
                                    layer=i, nb=nb, nt=nt, past_len=past_len)
            pool_p.append(hp)
            pool_s.append(jnp.transpose(hs, (1, 0, 2)))
            gate_v_s.append(jnp.transpose(vs.reshape(nt, nb, -1), (1, 0, 2)))
        else:
            bdw, lng, lnb = row(c_b_dw[i]), row(c_ln_g[i]), row(c_ln_b[i])
            xp, hp = _c_prompt(xp, gn, c_w_pw1, c_w_dw[i], bdw, lng, lnb, c_w_pw2,
                               layer=i, batch=batch, seq=seq, tm=ROW_TILE)
            xs, hs = _c_sample(xs, conv_t, gn, c_w_pw1, c_w_dw[i], bdw, lng, lnb, c_w_pw2, layer=i, nb=nb, nt=nt)
            conv_p.append(hp)
            conv_s.append(jnp.transpose(hs, (1, 0, 2)))
        gff = row(norm_ffn[layer])
        final = layer == depth - 1
        xp = _ffn(xp, gff, ffn_w1, ffn_w2, gf, layer=layer, final=final, tm=ROW_TILE)
        xs = _ffn(xs, gff, ffn_w1, ffn_w2, gf, layer=layer, final=final, tm=nt * nb)

    y_prompt = xp.reshape(batch, seq, d)
    y_sample = jnp.transpose(xs.reshape(nt, nb, d), (1, 0, 2))
    return (y_prompt, y_sample, jnp.stack(pool_p), jnp.stack(pool_s), jnp.stack(conv_p), jnp.stack(conv_s),
            jnp.stack(gate_v_s))
```

```python
import functools

import jax
import jax.numpy as jnp
from jax import lax
from jax.experimental import pallas as pl
from jax.experimental.pallas import tpu as pltpu

EPS = 1e-6
POOL_WINDOWS = (2, 4, 8, 16)
POOL_HIST = max(POOL_WINDOWS) - 1
GMLP_CHUNK = 128
CONV_WIDTH = 31
CONV_HIST = CONV_WIDTH - 1
PAST_LEN = 16384

LANES = 128
SUBLANES = 8
VMEM_LIMIT_BYTES = 56 * 1024 * 1024
ROW_TILE = 512
AB_ROW_TILE = 1024
CONV_ROW_BLOCK = 16
CONV_CHAINS = 2
POOL_HALO = 2 * SUBLANES
CONV_HALO = 4 * SUBLANES

F32 = jnp.float32
BF16 = jnp.bfloat16


def _rmsnorm(x, g):
    return x * lax.rsqrt(jnp.mean(x * x, axis=-1, keepdims=True) + EPS) * g


def _layernorm(x, g, b):
    mu = jnp.mean(x, axis=-1, keepdims=True)
    xc = x - mu
    var = jnp.mean(xc * xc, axis=-1, keepdims=True)
    return xc * lax.rsqrt(var + EPS) * g + b


def _gelu(x):
    return 0.5 * x * (1.0 + lax.erf(x * (2.0 ** -0.5)))


def _dot(a, b):
    return jnp.dot(a, b, preferred_element_type=F32)


def _resident(shape):
    zeros = (0,) * len(shape)
    return pl.BlockSpec(shape, lambda *_: zeros, pipeline_mode=pl.Buffered(1))


def _layer(stacked, layer):
    zeros = (0,) * (stacked.ndim - 1)
    return pl.BlockSpec((None,) + stacked.shape[1:], lambda *_: (layer,) + zeros, pipeline_mode=pl.Buffered(1))


def _params(n_grid_dims):
    return pltpu.CompilerParams(dimension_semantics=("arbitrary",) * n_grid_dims,
                                vmem_limit_bytes=VMEM_LIMIT_BYTES)


def _ffn_kernel(x_ref, g_ref, w1_ref, w2_ref, gf_ref, o_ref, *, final):
    x = x_ref[...]
    h = _rmsnorm(x, g_ref[...])
    a = _dot(h, w1_ref[...])
    a = jnp.square(jnp.maximum(a, 0.0))
    y = x + _dot(a, w2_ref[...])
    if final:
        y = _rmsnorm(y, gf_ref[...])
    o_ref[...] = y


def _ffn(x, g, w1, w2, gf, *, layer, final, tm):
    rows, d = x.shape
    return pl.pallas_call(
        functools.partial(_ffn_kernel, final=final),
        grid=(rows // tm,),
        in_specs=[pl.BlockSpec((tm, d), lambda i: (i, 0)),
                  _resident((1, d)), _layer(w1, layer), _layer(w2, layer), _resident((1, d))],
        out_specs=pl.BlockSpec((tm, d), lambda i: (i, 0)),
        out_shape=jax.ShapeDtypeStruct((rows, d), F32),
        compiler_params=_params(1),
        name="ffn",
    )(x, g, w1, w2, gf)


def _ab_project(x, gn, win, lng, lnb, d_pool, d_gmlp):
    h = _rmsnorm(x, gn)
    z = _dot(h, win)
    a = z[:, :d_pool]
    u = _gelu(z[:, d_pool:d_pool + d_gmlp])
    v = _gelu(z[:, d_pool + d_gmlp:])
    return a, u, _layernorm(v, lng, lnb)


def _pool_project(pooled, pw_ref, scale):
    group = pw_ref.shape[1]
    mixed = [_dot(pooled[:, g * group:(g + 1) * group], pw_ref[g])
             for g in range(pw_ref.shape[0])]
    return jnp.concatenate(mixed, axis=1) * scale


def _ab_prompt_kernel(x_ref, gn_ref, win_ref, pw_ref, ps_ref, lng_ref, lnb_ref, ws_ref, bst_ref, wout_ref,
                      y_ref, hist_ref, aext_ref, *, tm):
    j = pl.program_id(1)
    d_pool = ps_ref.shape[1]
    d_gmlp = lng_ref.shape[1]
    n_heads = ws_ref.shape[0]
    head = d_gmlp // n_heads
    group = d_pool // len(POOL_WINDOWS)
    halo = POOL_HALO

    @pl.when(j == 0)
    def _():
        aext_ref[0:halo, :] = jnp.zeros((halo, d_pool), F32)

    @pl.when(j > 0)
    def _():
        aext_ref[0:halo, :] = aext_ref[tm:tm + halo, :]

    x = x_ref[...]
    a, u, v = _ab_project(x, gn_ref[...], win_ref[...], lng_ref[...], lnb_ref[...], d_pool, d_gmlp)
    aext_ref[halo:halo + tm, :] = a
    hist_ref[...] = a[tm - POOL_HIST:, :]

    pos = j * tm + lax.broadcasted_iota(jnp.int32, (tm, 1), 0)
    pooled = []
    for g, w in enumerate(POOL_WINDOWS):
        cols = slice(g * group, (g + 1) * group)
        s = a[:, cols]
        for k in range(1, w):
            s = s + aext_ref[halo - k:halo - k + tm, cols]
        cnt = jnp.minimum(pos + 1, w).astype(F32)
        pooled.append(s / cnt - a[:, cols])
    pool_out = _pool_project(jnp.concatenate(pooled, axis=1), pw_ref, ps_ref[...])

    n_chunks = tm // GMLP_CHUNK
    tril = (lax.broadcasted_iota(jnp.int32, (GMLP_CHUNK, GMLP_CHUNK), 0)
            >= lax.broadcasted_iota(jnp.int32, (GMLP_CHUNK, GMLP_CHUNK), 1))
    mixed = []
    for h in range(n_heads):
        wm = jnp.where(tril, ws_ref[h], 0.0)
        rhs = jnp.concatenate([v[c * GMLP_CHUNK:(c + 1) * GMLP_CHUNK, h * head:(h + 1) * head]
                               for c in range(n_chunks)], axis=1)
        mixed.append(_dot(wm, rhs) + bst_ref[:, h:h + 1])
    gate = jnp.concatenate(
        [jnp.concatenate([mixed[h][:, c * head:(c + 1) * head] for h in range(n_heads)], axis=1)
         for c in range(n_chunks)], axis=0)

    cat = jnp.concatenate([pool_out, u * gate], axis=1)
    y_ref[...] = x + _dot(cat, wout_ref[...])


def _ab_prompt(x, gn, win, pw, ps, lng, lnb, ws, bst, wout, *, layer, batch, seq, tm):
    d = x.shape[1]
    d_pool = ps.shape[1]
    assert tm % GMLP_CHUNK == 0 and seq % tm == 0
    tiles = seq // tm
    weights = (gn, win, pw, ps, lng, lnb, ws, bst, wout)
    stacked = (win, pw, ws, wout)
    return pl.pallas_call(
        functools.partial(_ab_prompt_kernel, tm=tm),
        grid=(batch, tiles),
        in_specs=[pl.BlockSpec((tm, d), lambda b, j: (b * tiles + j, 0))]
                 + [_layer(w, layer) if any(w is s for s in stacked) else _resident(w.shape) for w in weights],
        out_specs=[pl.BlockSpec((tm, d), lambda b, j: (b * tiles + j, 0)),
                   pl.BlockSpec((None, POOL_HIST, d_pool), lambda b, j: (b, 0, 0))],
        out_shape=[jax.ShapeDtypeStruct((batch * seq, d), F32),
                   jax.ShapeDtypeStruct((batch, POOL_HIST, d_pool), F32)],
        scratch_shapes=[pltpu.VMEM((POOL_HALO + tm, d_pool), F32)],
        compiler_params=_params(2),
        name="ab_prompt",
    )(x, *weights)


def _ab_sample_kernel(x_ref, hist_ref, gn_ref, win_ref, pw_ref, ps_ref, lng_ref, lnb_ref, coef_ref, bias_ref,
                      wout_ref, y_ref, nhist_ref, v_ref, *, nb, nt, past_len):
    d_pool = ps_ref.shape[1]
    d_gmlp = lng_ref.shape[1]
    group = d_pool // len(POOL_WINDOWS)

    x = x_ref[...]
    a, u, v = _ab_project(x, gn_ref[...], win_ref[...], lng_ref[...], lnb_ref[...], d_pool, d_gmlp)
    v_ref[...] = v

    def ext(k):
        return hist_ref[k] if k < POOL_HIST else a[(k - POOL_HIST) * nb:(k - POOL_HIST + 1) * nb, :]

    for k in range(POOL_HIST):
        nhist_ref[k] = ext(k + nt)

    pooled, gate = [], []
    for t in range(nt):
        groups = []
        for g, w in enumerate(POOL_WINDOWS):
            cols = slice(g * group, (g + 1) * group)
            s = ext(POOL_HIST + t)[:, cols]
            for k in range(1, w):
                s = s + ext(POOL_HIST + t - k)[:, cols]
            cnt = float(min(past_len + t + 1, w))
            groups.append(s / cnt - a[t * nb:(t + 1) * nb, cols])
        pooled.append(jnp.concatenate(groups, axis=1))
        m = bias_ref[t:t + 1, :]
        for s_ in range(t + 1):
            vs = v[s_ * nb:(s_ + 1) * nb, :].astype(BF16).astype(F32)
            m = m + coef_ref[t * nt + s_:t * nt + s_ + 1, :].astype(BF16).astype(F32) * vs
        gate.append(m)
    pool_out = _pool_project(jnp.concatenate(pooled, axis=0), pw_ref, ps_ref[...])
    cat = jnp.concatenate([pool_out, u * jnp.concatenate(gate, axis=0)], axis=1)
    y_ref[...] = x + _dot(cat, wout_ref[...])


def _ab_sample(x, hist_t, gn, win, pw, ps, lng, lnb, coef, bias, wout, *, layer, nb, nt, past_len):
    rows, d = x.shape
    d_pool = ps.shape[1]
    d_gmlp = lng.shape[1]
    operands = (x, hist_t, gn, win, pw, ps, lng, lnb, coef, bias, wout)
    stacked = (hist_t, win, pw, wout)
    return pl.pallas_call(
        functools.partial(_ab_sample_kernel, nb=nb, nt=nt, past_len=past_len),
        grid=(1,),
        in_specs=[_layer(o, layer) if any(o is s for s in stacked) else _resident(o.shape) for o in operands],
        out_specs=[_resident((rows, d)), _resident((POOL_HIST, nb, d_pool)), _resident((rows, d_gmlp))],
        out_shape=[jax.ShapeDtypeStruct((rows, d), F32),
                   jax.ShapeDtypeStruct((POOL_HIST, nb, d_pool), F32),
                   jax.ShapeDtypeStruct((rows, d_gmlp), F32)],
        compiler_params=_params(1),
        name="ab_sample",
    )(*operands)


def _glu_project(x, gn, wpw1, d_conv):
    h = _rmsnorm(x, gn)
    z = _dot(h, wpw1)
    return z[:, :d_conv] * jax.nn.sigmoid(z[:, d_conv:])


def _conv_tail(c, lng, lnb, wpw2):
    c = _layernorm(c, lng, lnb)
    return _dot(c * jax.nn.sigmoid(c), wpw2)


def _c_prompt_kernel(x_ref, gn_ref, wpw1_ref, wdw_ref, bdw_ref, lng_ref, lnb_ref, wpw2_ref,
                     y_ref, hist_ref, ext_ref, conv_ref, *, tm):
    j = pl.program_id(1)
    d_conv = bdw_ref.shape[1]
    halo = CONV_HALO
    lead = halo - CONV_HIST
    n_pairs = d_conv // (2 * LANES)

    @pl.when(j == 0)
    def _():
        ext_ref[:, 0:2 * halo, :] = jnp.zeros((n_pairs, 2 * halo, LANES), F32)

    @pl.when(j > 0)
    def _():
        ext_ref[:, 0:2 * halo, :] = ext_ref[:, 2 * tm:2 * (tm + halo), :]

    x = x_ref[...]
    glu = _glu_project(x, gn_ref[...], wpw1_ref[...], d_conv)
    hist_ref[...] = glu[tm - CONV_HIST:, :]
    for cb in range(d_conv // LANES):
        ext_ref[cb // 2, pl.ds(2 * halo + cb % 2, tm, stride=2), :] = glu[:, cb * LANES:(cb + 1) * LANES]

    rb = CONV_ROW_BLOCK
    for cb in range(d_conv // LANES):
        cols = slice(cb * LANES, (cb + 1) * LANES)
        w = [jnp.broadcast_to(wdw_ref[k:k + 1, cols], (rb, LANES)) for k in range(CONV_WIDTH)]
        b = jnp.broadcast_to(bdw_ref[:, cols], (rb, LANES))
        for r in range(0, tm, rb):
            chains = [b] + [None] * (CONV_CHAINS - 1)
            for k in range(CONV_WIDTH):
                term = w[k] * ext_ref[cb // 2, pl.ds(2 * (lead + r + k) + cb % 2, rb, stride=2), :]
                c = k % CONV_CHAINS
                chains[c] = term if chains[c] is None else chains[c] + term
            conv_ref[r:r + rb, cols] = functools.reduce(lambda p, q: p + q, chains)

    y_ref[...] = x + _conv_tail(conv_ref[...], lng_ref[...], lnb_ref[...], wpw2_ref[...])


def _c_prompt(x, gn, wpw1, wdw, bdw, lng, lnb, wpw2, *, layer, batch, seq, tm):
    d = x.shape[1]
    d_conv = bdw.shape[1]
    assert seq % tm == 0 and tm % CONV_ROW_BLOCK == 0 and d_conv % (2 * LANES) == 0
    tiles = seq // tm
    weights = (gn, wpw1, wdw, bdw, lng, lnb, wpw2)
    stacked = (wpw1, wpw2)
    return pl.pallas_call(
        functools.partial(_c_prompt_kernel, tm=tm),
        grid=(batch, tiles),
        in_specs=[pl.BlockSpec((tm, d), lambda b, j: (b * tiles + j, 0))]
                 + [_layer(w, layer) if any(w is s for s in stacked) else _resident(w.shape) for w in weights],
        out_specs=[pl.BlockSpec((tm, d), lambda b, j: (b * tiles + j, 0)),
                   pl.BlockSpec((None, CONV_HIST, d_conv), lambda b, j: (b, 0, 0))],
        out_shape=[jax.ShapeDtypeStruct((batch * seq, d), F32),
                   jax.ShapeDtypeStruct((batch, CONV_HIST, d_conv), F32)],
        scratch_shapes=[pltpu.VMEM((d_conv // (2 * LANES), 2 * (CONV_HALO + tm), LANES), F32),
                        pltpu.VMEM((tm, d_conv), F32)],
        compiler_params=_params(2),
        name="c_prompt",
    )(x, *weights)


def _c_sample_kernel(x_ref, hist_ref, gn_ref, wpw1_ref, wdw_ref, bdw_ref, lng_ref, lnb_ref, wpw2_ref,
                     y_ref, nhist_ref, conv_ref, *, nb, nt):
    d_conv = bdw_ref.shape[1]
    x = x_ref[...]
    glu = _glu_project(x, gn_ref[...], wpw1_ref[...], d_conv)

    def ext(k):
        return hist_ref[k] if k < CONV_HIST else glu[(k - CONV_HIST) * nb:(k - CONV_HIST + 1) * nb, :]

    for k in range(CONV_HIST):
        nhist_ref[k] = ext(k + nt)

    for t in range(nt):
        acc = jnp.broadcast_to(bdw_ref[...], (nb, d_conv))
        for k in range(CONV_WIDTH):
            acc = acc + wdw_ref[k:k + 1, :] * ext(t + k)
        conv_ref[t * nb:(t + 1) * nb, :] = acc

    y_ref[...] = x + _conv_tail(conv_ref[...], lng_ref[...], lnb_ref[...], wpw2_ref[...])


def _c_sample(x, hist_t, gn, wpw1, wdw, bdw, lng, lnb, wpw2, *, layer, nb, nt):
    rows, d = x.shape
    d_conv = bdw.shape[1]
    operands = (x, hist_t, gn, wpw1, wdw, bdw, lng, lnb, wpw2)
    stacked = (hist_t, wpw1, wpw2)
    return pl.pallas_call(
        functools.partial(_c_sample_kernel, nb=nb, nt=nt),
        grid=(1,),
        in_specs=[_layer(o, layer) if any(o is s for s in stacked) else _resident(o.shape) for o in operands],
        out_specs=[_resident((rows, d)), _resident((CONV_HIST, nb, d_conv))],
        out_shape=[jax.ShapeDtypeStruct((rows, d), F32),
                   jax.ShapeDtypeStruct((CONV_HIST, nb, d_conv), F32)],
        scratch_shapes=[pltpu.VMEM((rows, d_conv), F32)],
        compiler_params=_params(1),
        name="c_sample",
    )(*operands)


def kernel(x_prompt, x_sample, state_pool, state_conv, norm_mix, norm_ffn, norm_final, ab_w_in, ab_pool_w, ab_pool_scale, ab_ln_g, ab_ln_b, ab_ws, ab_bs, ab_w_out, c_w_pw1, c_w_dw, c_b_dw, c_ln_g, c_ln_b, c_w_pw2, ffn_w1, ffn_w2):
    batch, seq, d = x_prompt.shape
    nb, nt, _ = x_sample.shape
    depth = norm_mix.shape[0]
    past_len = PAST_LEN
    assert past_len % GMLP_CHUNK == 0 and nt <= GMLP_CHUNK

    row = lambda v: v.reshape(1, -1)
    xp = x_prompt.reshape(batch * seq, d)
    xs = jnp.transpose(x_sample, (1, 0, 2)).reshape(nt * nb, d)
    gf = row(norm_final)
    pool_t = jnp.transpose(state_pool, (0, 2, 1, 3))
    conv_t = jnp.transpose(state_conv, (0, 2, 1, 3))

    pool_p, pool_s, conv_p, conv_s, gate_v_s = [], [], [], [], []
    for layer in range(depth):
        i = layer // 2
        gn = row(norm_mix[layer])
        if layer % 2 == 0:
            ps, lng, lnb = row(ab_pool_scale[i]), row(ab_ln_g[i]), row(ab_ln_b[i])
            head = lng.shape[1] // ab_ws.shape[1]
            xp, hp = _ab_prompt(xp, gn, ab_w_in, ab_pool_w, ps, lng, lnb, ab_ws, ab_bs[i].T, ab_w_out,
                                layer=i, batch=batch, seq=seq, tm=AB_ROW_TILE)
            coef = jnp.repeat(jnp.transpose(ab_ws[i][:, :nt, :nt], (1, 2, 0)).reshape(nt * nt, -1), head, axis=1)
            bias = jnp.repeat(ab_bs[i][:, :nt].T, head, axis=1)
            xs, hs, vs = _ab_sample(xs, pool_t, gn, ab_w_in, ab_pool_w, ps, lng, lnb, coef, bias, ab_w_out,
                                    layer=i, nb=nb, nt=nt, past_len=past_len)
            pool_p.append(hp)
            pool_s.append(jnp.transpose(hs, (1, 0, 2)))
            gate_v_s.append(jnp.transpose(vs.reshape(nt, nb, -1), (1, 0, 2)))
        else:
            bdw, lng, lnb = row(c_b_dw[i]), row(c_ln_g[i]), row(c_ln_b[i])
            xp, hp = _c_prompt(xp, gn, c_w_pw1, c_w_dw[i], bdw, lng, lnb, c_w_pw2,
                               layer=i, batch=batch, seq=seq, tm=ROW_TILE)
            xs, hs = _c_sample(xs, conv_t, gn, c_w_pw1, c_w_dw[i], bdw, lng, lnb, c_w_pw2, layer=i, nb=nb, nt=nt)
            conv_p.append(hp)
            conv_s.append(jnp.transpose(hs, (1, 0, 2)))
        gff = row(norm_ffn[layer])
        final = layer == depth - 1
        xp = _ffn(xp, gff, ffn_w1, ffn_w2, gf, layer=layer, final=final, tm=ROW_TILE)
        xs = _ffn(xs, gff, ffn_w1, ffn_w2, gf, layer=layer, final=final, tm=nt * nb)

    y_prompt = xp.reshape(batch, seq, d)
    y_sample = jnp.transpose(xs.reshape(nt, nb, d), (1, 0, 2))
    return (y_prompt, y_sample, jnp.stack(pool_p), jnp.stack(pool_s), jnp.stack(conv_p), jnp.stack(conv_s),
            jnp.stack(gate_v_s))
```

```python
import functools

import jax
import jax.numpy as jnp
from jax import lax
from jax.experimental import pallas as pl
from jax.experimental.pallas import tpu as pltpu

EPS = 1e-6
POOL_WINDOWS = (2, 4, 8, 16)
POOL_HIST = max(POOL_WINDOWS) - 1
GMLP_CHUNK = 128
CONV_WIDTH = 31
CONV_HIST = CONV_WIDTH - 1
PAST_LEN = 16384

LANES = 128
SUBLANES = 8
VMEM_LIMIT_BYTES = 56 * 1024 * 1024
ROW_TILE = 512
AB_ROW_TILE = 1024
CONV_ROW_BLOCK = 16
CONV_CHAINS = 2
GLU_COLUMN_CHUNKS = 4
POOL_HALO = 2 * SUBLANES
CONV_HALO = 4 * SUBLANES

F32 = jnp.float32
BF16 = jnp.bfloat16


def _rmsnorm(x, g):
    return x * lax.rsqrt(jnp.mean(x * x, axis=-1, keepdims=True) + EPS) * g


def _layernorm(x, g, b):
    mu = jnp.mean(x, axis=-1, keepdims=True)
    xc = x - mu
    var = jnp.mean(xc * xc, axis=-1, keepdims=True)
    return xc * lax.rsqrt(var + EPS) * g + b


def _gelu(x):
    return 0.5 * x * (1.0 + lax.erf(x * (2.0 ** -0.5)))


def _dot(a, b):
    return jnp.dot(a, b, preferred_element_type=F32)


def _resident(shape):
    zeros = (0,) * len(shape)
    return pl.BlockSpec(shape, lambda *_: zeros, pipeline_mode=pl.Buffered(1))


def _layer(stacked, layer):
    zeros = (0,) * (stacked.ndim - 1)
    return pl.BlockSpec((None,) + stacked.shape[1:], lambda *_: (layer,) + zeros, pipeline_mode=pl.Buffered(1))


def _params(n_grid_dims):
    return pltpu.CompilerParams(dimension_semantics=("arbitrary",) * n_grid_dims,
                                vmem_limit_bytes=VMEM_LIMIT_BYTES)


def _ffn_kernel(x_ref, g_ref, w1_ref, w2_ref, gf_ref, o_ref, *, final):
    x = x_ref[...]
    h = _rmsnorm(x, g_ref[...])
    a = _dot(h, w1_ref[...])
    a = jnp.square(jnp.maximum(a, 0.0))
    y = x + _dot(a, w2_ref[...])
    if final:
        y = _rmsnorm(y, gf_ref[...])
    o_ref[...] = y


def _ffn(x, g, w1, w2, gf, *, layer, final, tm):
    rows, d = x.shape
    return pl.pallas_call(
        functools.partial(_ffn_kernel, final=final),
        grid=(rows // tm,),
        in_specs=[pl.BlockSpec((tm, d), lambda i: (i, 0)),
                  _resident((1, d)), _layer(w1, layer), _layer(w2, layer), _resident((1, d))],
        out_specs=pl.BlockSpec((tm, d), lambda i: (i, 0)),
        out_shape=jax.ShapeDtypeStruct((rows, d), F32),
        compiler_params=_params(1),
        name="ffn",
    )(x, g, w1, w2, gf)


def _ab_project(x, gn, win, lng, lnb, d_pool, d_gmlp):
    h = _rmsnorm(x, gn)
    z = _dot(h, win)
    a = z[:, :d_pool]
    u = _gelu(z[:, d_pool:d_pool + d_gmlp])
    v = _gelu(z[:, d_pool + d_gmlp:])
    return a, u, _layernorm(v, lng, lnb)


def _pool_project(pooled, pw_ref, scale):
    group = pw_ref.shape[1]
    mixed = [_dot(pooled[:, g * group:(g + 1) * group], pw_ref[g])
             for g in range(pw_ref.shape[0])]
    return jnp.concatenate(mixed, axis=1) * scale


def _ab_prompt_kernel(x_ref, gn_ref, win_ref, pw_ref, ps_ref, lng_ref, lnb_ref, ws_ref, bst_ref, wout_ref,
                      y_ref, hist_ref, aext_ref, *, tm):
    j = pl.program_id(1)
    d_pool = ps_ref.shape[1]
    d_gmlp = lng_ref.shape[1]
    n_heads = ws_ref.shape[0]
    head = d_gmlp // n_heads
    group = d_pool // len(POOL_WINDOWS)
    halo = POOL_HALO

    @pl.when(j == 0)
    def _():
        aext_ref[0:halo, :] = jnp.zeros((halo, d_pool), F32)

    @pl.when(j > 0)
    def _():
        aext_ref[0:halo, :] = aext_ref[tm:tm + halo, :]

    x = x_ref[...]
    a, u, v = _ab_project(x, gn_ref[...], win_ref[...], lng_ref[...], lnb_ref[...], d_pool, d_gmlp)
    aext_ref[halo:halo + tm, :] = a
    hist_ref[...] = a[tm - POOL_HIST:, :]

    pos = j * tm + lax.broadcasted_iota(jnp.int32, (tm, 1), 0)
    pooled = []
    for g, w in enumerate(POOL_WINDOWS):
        cols = slice(g * group, (g + 1) * group)
        s = a[:, cols]
        for k in range(1, w):
            s = s + aext_ref[halo - k:halo - k + tm, cols]
        cnt = jnp.minimum(pos + 1, w).astype(F32)
        pooled.append(s / cnt - a[:, cols])
    pool_out = _pool_project(jnp.concatenate(pooled, axis=1), pw_ref, ps_ref[...])

    n_chunks = tm // GMLP_CHUNK
    tril = (lax.broadcasted_iota(jnp.int32, (GMLP_CHUNK, GMLP_CHUNK), 0)
            >= lax.broadcasted_iota(jnp.int32, (GMLP_CHUNK, GMLP_CHUNK), 1))
    mixed = []
    for h in range(n_heads):
        wm = jnp.where(tril, ws_ref[h], 0.0)
        rhs = jnp.concatenate([v[c * GMLP_CHUNK:(c + 1) * GMLP_CHUNK, h * head:(h + 1) * head]
                               for c in range(n_chunks)], axis=1)
        mixed.append(_dot(wm, rhs) + bst_ref[:, h:h + 1])
    gate = jnp.concatenate(
        [jnp.concatenate([mixed[h][:, c * head:(c + 1) * head] for h in range(n_heads)], axis=1)
         for c in range(n_chunks)], axis=0)

    cat = jnp.concatenate([pool_out, u * gate], axis=1)
    y_ref[...] = x + _dot(cat, wout_ref[...])


def _ab_prompt(x, gn, win, pw, ps, lng, lnb, ws, bst, wout, *, layer, batch, seq, tm):
    d = x.shape[1]
    d_pool = ps.shape[1]
    assert tm % GMLP_CHUNK == 0 and seq % tm == 0
    tiles = seq // tm
    weights = (gn, win, pw, ps, lng, lnb, ws, bst, wout)
    stacked = (win, pw, ws, wout)
    return pl.pallas_call(
        functools.partial(_ab_prompt_kernel, tm=tm),
        grid=(batch, tiles),
        in_specs=[pl.BlockSpec((tm, d), lambda b, j: (b * tiles + j, 0))]
                 + [_layer(w, layer) if any(w is s for s in stacked) else _resident(w.shape) for w in weights],
        out_specs=[pl.BlockSpec((tm, d), lambda b, j: (b * tiles + j, 0)),
                   pl.BlockSpec((None, POOL_HIST, d_pool), lambda b, j: (b, 0, 0))],
        out_shape=[jax.ShapeDtypeStruct((batch * seq, d), F32),
                   jax.ShapeDtypeStruct((batch, POOL_HIST, d_pool), F32)],
        scratch_shapes=[pltpu.VMEM((POOL_HALO + tm, d_pool), F32)],
        compiler_params=_params(2),
        name="ab_prompt",
    )(x, *weights)


def _ab_sample_kernel(x_ref, hist_ref, gn_ref, win_ref, pw_ref, ps_ref, lng_ref, lnb_ref, coef_ref, bias_ref,
                      wout_ref, y_ref, nhist_ref, v_ref, *, nb, nt, past_len):
    d_pool = ps_ref.shape[1]
    d_gmlp = lng_ref.shape[1]
    group = d_pool // len(POOL_WINDOWS)

    x = x_ref[...]
    a, u, v = _ab_project(x, gn_ref[...], win_ref[...], lng_ref[...], lnb_ref[...], d_pool, d_gmlp)
    v_ref[...] = v

    def ext(k):
        return hist_ref[k] if k < POOL_HIST else a[(k - POOL_HIST) * nb:(k - POOL_HIST + 1) * nb, :]

    for k in range(POOL_HIST):
        nhist_ref[k] = ext(k + nt)

    pooled, gate = [], []
    for t in range(nt):
        groups = []
        for g, w in enumerate(POOL_WINDOWS):
            cols = slice(g * group, (g + 1) * group)
            s = ext(POOL_HIST + t)[:, cols]
            for k in range(1, w):
                s = s + ext(POOL_HIST + t - k)[:, cols]
            cnt = float(min(past_len + t + 1, w))
            groups.append(s / cnt - a[t * nb:(t + 1) * nb, cols])
        pooled.append(jnp.concatenate(groups, axis=1))
        m = bias_ref[t:t + 1, :]
        for s_ in range(t + 1):
            vs = v[s_ * nb:(s_ + 1) * nb, :].astype(BF16).astype(F32)
            m = m + coef_ref[t * nt + s_:t * nt + s_ + 1, :].astype(BF16).astype(F32) * vs
        gate.append(m)
    pool_out = _pool_project(jnp.concatenate(pooled, axis=0), pw_ref, ps_ref[...])
    cat = jnp.concatenate([pool_out, u * jnp.concatenate(gate, axis=0)], axis=1)
    y_ref[...] = x + _dot(cat, wout_ref[...])


def _ab_sample(x, hist_t, gn, win, pw, ps, lng, lnb, coef, bias, wout, *, layer, nb, nt, past_len):
    rows, d = x.shape
    d_pool = ps.shape[1]
    d_gmlp = lng.shape[1]
    operands = (x, hist_t, gn, win, pw, ps, lng, lnb, coef, bias, wout)
    stacked = (hist_t, win, pw, wout)
    return pl.pallas_call(
        functools.partial(_ab_sample_kernel, nb=nb, nt=nt, past_len=past_len),
        grid=(1,),
        in_specs=[_layer(o, layer) if any(o is s for s in stacked) else _resident(o.shape) for o in operands],
        out_specs=[_resident((rows, d)), _resident((POOL_HIST, nb, d_pool)), _resident((rows, d_gmlp))],
        out_shape=[jax.ShapeDtypeStruct((rows, d), F32),
                   jax.ShapeDtypeStruct((POOL_HIST, nb, d_pool), F32),
                   jax.ShapeDtypeStruct((rows, d_gmlp), F32)],
        compiler_params=_params(1),
        name="ab_sample",
    )(*operands)


def _glu_project(x, gn, wpw1, d_conv):
    h = _rmsnorm(x, gn)
    z = _dot(h, wpw1)
    return z[:, :d_conv] * jax.nn.sigmoid(z[:, d_conv:])


def _conv_tail(c, lng, lnb, wpw2):
    c = _layernorm(c, lng, lnb)
    return _dot(c * jax.nn.sigmoid(c), wpw2)


def _c_prompt_kernel(x_ref, gn_ref, wpw1_ref, wdw_ref, bdw_ref, lng_ref, lnb_ref, wpw2_ref,
                     y_ref, hist_ref, ext_ref, conv_ref, *, tm):
    j = pl.program_id(1)
    d_conv = bdw_ref.shape[1]
    halo = CONV_HALO
    lead = halo - CONV_HIST
    n_pairs = d_conv // (2 * LANES)

    @pl.when(j == 0)
    def _():
        ext_ref[:, 0:2 * halo, :] = jnp.zeros((n_pairs, 2 * halo, LANES), F32)

    @pl.when(j > 0)
    def _():
        ext_ref[:, 0:2 * halo, :] = ext_ref[:, 2 * tm:2 * (tm + halo), :]

    x = x_ref[...]
    h = _rmsnorm(x, gn_ref[...])
    rb = CONV_ROW_BLOCK
    width = d_conv // GLU_COLUMN_CHUNKS
    for c0 in range(0, d_conv, width):
        glu = _dot(h, wpw1_ref[:, c0:c0 + width]) * jax.nn.sigmoid(_dot(h, wpw1_ref[:, d_conv + c0:d_conv + c0 + width]))
        hist_ref[:, c0:c0 + width] = glu[tm - CONV_HIST:, :]
        blocks = range(c0 // LANES, (c0 + width) // LANES)
        for cb in blocks:
            ext_ref[cb // 2, pl.ds(2 * halo + cb % 2, tm, stride=2), :] = glu[:, cb * LANES - c0:(cb + 1) * LANES - c0]
        for cb in blocks:
            cols = slice(cb * LANES, (cb + 1) * LANES)
            w = [jnp.broadcast_to(wdw_ref[k:k + 1, cols], (rb, LANES)) for k in range(CONV_WIDTH)]
            b = jnp.broadcast_to(bdw_ref[:, cols], (rb, LANES))
            for r in range(0, tm, rb):
                chains = [b] + [None] * (CONV_CHAINS - 1)
                for k in range(CONV_WIDTH):
                    term = w[k] * ext_ref[cb // 2, pl.ds(2 * (lead + r + k) + cb % 2, rb, stride=2), :]
                    c = k % CONV_CHAINS
                    chains[c] = term if chains[c] is None else chains[c] + term
                conv_ref[r:r + rb, cols] = functools.reduce(lambda p, q: p + q, chains)

    y_ref[...] = x + _conv_tail(conv_ref[...], lng_ref[...], lnb_ref[...], wpw2_ref[...])


def _c_prompt(x, gn, wpw1, wdw, bdw, lng, lnb, wpw2, *, layer, batch, seq, tm):
    d = x.shape[1]
    d_conv = bdw.shape[1]
    assert seq % tm == 0 and tm % CONV_ROW_BLOCK == 0 and d_conv % (2 * LANES * GLU_COLUMN_CHUNKS) == 0
    tiles = seq // tm
    weights = (gn, wpw1, wdw, bdw, lng, lnb, wpw2)
    stacked = (wpw1, wpw2)
    return pl.pallas_call(
        functools.partial(_c_prompt_kernel, tm=tm),
        grid=(batch, tiles),
        in_specs=[pl.BlockSpec((tm, d), lambda b, j: (b * tiles + j, 0))]
                 + [_layer(w, layer) if any(w is s for s in stacked) else _resident(w.shape) for w in weights],
        out_specs=[pl.BlockSpec((tm, d), lambda b, j: (b * tiles + j, 0)),
                   pl.BlockSpec((None, CONV_HIST, d_conv), lambda b, j: (b, 0, 0))],
        out_shape=[jax.ShapeDtypeStruct((batch * seq, d), F32),
                   jax.ShapeDtypeStruct((batch, CONV_HIST, d_conv), F32)],
        scratch_shapes=[pltpu.VMEM((d_conv // (2 * LANES), 2 * (CONV_HALO + tm), LANES), F32),
                        pltpu.VMEM((tm, d_conv), F32)],
        compiler_params=_params(2),
        name="c_prompt",
    )(x, *weights)


def _c_sample_kernel(x_ref, hist_ref, gn_ref, wpw1_ref, wdw_ref, bdw_ref, lng_ref, lnb_ref, wpw2_ref,
                     y_ref, nhist_ref, conv_ref, *, nb, nt):
    d_conv = bdw_ref.shape[1]
    x = x_ref[...]
    glu = _glu_project(x, gn_ref[...], wpw1_ref[...], d_conv)

    def ext(k):
        return hist_ref[k] if k < CONV_HIST else glu[(k - CONV_HIST) * nb:(k - CONV_HIST + 1) * nb, :]

    for k in range(CONV_HIST):
        nhist_ref[k] = ext(k + nt)

    for t in range(nt):
        acc = jnp.broadcast_to(bdw_ref[...], (nb, d_conv))
        for k in range(CONV_WIDTH):
            acc = acc + wdw_ref[k:k + 1, :] * ext(t + k)
        conv_ref[t * nb:(t + 1) * nb, :] = acc

    y_ref[...] = x + _conv_tail(conv_ref[...], lng_ref[...], lnb_ref[...], wpw2_ref[...])


def _c_sample(x, hist_t, gn, wpw1, wdw, bdw, lng, lnb, wpw2, *, layer, nb, nt):
    rows, d = x.shape
    d_conv = bdw.shape[1]
    operands = (x, hist_t, gn, wpw1, wdw, bdw, lng, lnb, wpw2)
    stacked = (hist_t, wpw1, wpw2)
    return pl.pallas_call(
        functools.partial(_c_sample_kernel, nb=nb, nt=nt),
        grid=(1,),
        in_specs=[_layer(o, layer) if any(o is s for s in stacked) else _resident(o.shape) for o in operands],
        out_specs=[_resident((rows, d)), _resident((CONV_HIST, nb, d_conv))],
        out_shape=[jax.ShapeDtypeStruct((rows, d), F32),
                   jax.ShapeDtypeStruct((CONV_HIST, nb, d_conv), F32)],
        scratch_shapes=[pltpu.VMEM((rows, d_conv), F32)],
        compiler_params=_params(1),
        name="c_sample",
    )(*operands)


def kernel(x_prompt, x_sample, state_pool, state_conv, norm_mix, norm_ffn, norm_final, ab_w_in, ab_pool_w, ab_pool_scale, ab_ln_g, ab_ln_b, ab_ws, ab_bs, ab_w_out, c_w_pw1, c_w_dw, c_b_dw, c_ln_g, c_ln_b, c_w_pw2, ffn_w1, ffn_w2):
    batch, seq, d = x_prompt.shape
    nb, nt, _ = x_sample.shape
    depth = norm_mix.shape[0]
    past_len = PAST_LEN
    assert past_len % GMLP_CHUNK == 0 and nt <= GMLP_CHUNK

    row = lambda v: v.reshape(1, -1)
    xp = x_prompt.reshape(batch * seq, d)
    xs = jnp.transpose(x_sample, (1, 0, 2)).reshape(nt * nb, d)
    gf = row(norm_final)
    pool_t = jnp.transpose(state_pool, (0, 2, 1, 3))
    conv_t = jnp.transpose(state_conv, (0, 2, 1, 3))

    pool_p, pool_s, conv_p, conv_s, gate_v_s = [], [], [], [], []
    for layer in range(depth):
        i = layer // 2
        gn = row(norm_mix[layer])
        if layer % 2 == 0:
            ps, lng, lnb = row(ab_pool_scale[i]), row(ab_ln_g[i]), row(ab_ln_b[i])
            head = lng.shape[1] // ab_ws.shape[1]
            xp, hp = _ab_prompt(xp, gn, ab_w_in, ab_pool_w, ps, lng, lnb, ab_ws, ab_bs[i].T, ab_w_out,
                                layer=i, batch=batch, seq=seq, tm=AB_ROW_TILE)
            coef = jnp.repeat(jnp.transpose(ab_ws[i][:, :nt, :nt], (1, 2, 0)).reshape(nt * nt, -1), head, axis=1)
            bias = jnp.repeat(ab_bs[i][:, :nt].T, head, axis=1)
            xs, hs, vs = _ab_sample(xs, pool_t, gn, ab_w_in, ab_pool_w, ps, lng, lnb, coef, bias, ab_w_out,
                                    layer=i, nb=nb, nt=nt, past_len=past_len)
            pool_p.append(hp)
            pool_s.append(jnp.transpose(hs, (1, 0, 2)))
            gate_v_s.append(jnp.transpose(vs.reshape(nt, nb, -1), (1, 0, 2)))
        else:
            bdw, lng, lnb = row(c_b_dw[i]), row(c_ln_g[i]), row(c_ln_b[i])
            xp, hp = _c_prompt(xp, gn, c_w_pw1, c_w_dw[i], bdw, lng, lnb, c_w_pw2,
                               layer=i, batch=batch, seq=seq, tm=ROW_TILE)
            xs, hs = _c_sample(xs, conv_t, gn, c_w_pw1, c_w_dw[i], bdw, lng, lnb, c_w_pw2, layer=i, nb=nb, nt=nt)
            conv_p.append(hp)
            conv_s.append(jnp.transpose(hs, (1, 0, 2)))
        gff = row(norm_ffn[layer])
        final = layer == depth - 1
        xp = _ffn(xp, gff, ffn_w1, ffn_w2, gf, layer=layer, final=final, tm=ROW_TILE)
        xs = _ffn(xs, gff, ffn_w1, ffn_w2, gf, layer=layer, final=final, tm=nt * nb)

    y_prompt = xp.reshape(batch, seq, d)
    y_sample = jnp.transpose(xs.reshape(nt, nb, d), (1, 0, 2))
    return (y_prompt, y_sample, jnp.stack(pool_p), jnp.stack(pool_s), jnp.stack(conv_p), jnp.stack(conv_s),
            jnp.stack(gate_v_s))
```

```python
import functools

import jax
import jax.numpy as jnp
from jax import lax
from jax.experimental import pallas as pl
from jax.experimental.pallas import tpu as pltpu

EPS = 1e-6
POOL_WINDOWS = (2, 4, 8, 16)
POOL_HIST = max(POOL_WINDOWS) - 1
GMLP_CHUNK = 128
CONV_WIDTH = 31
CONV_HIST = CONV_WIDTH - 1
PAST_LEN = 16384

LANES = 128
SUBLANES = 8
VMEM_LIMIT_BYTES = 56 * 1024 * 1024
ROW_TILE = 512
AB_ROW_TILE = 1024
CONV_ROW_BLOCK = 16
CONV_CHAINS = 2
GLU_COLUMN_CHUNKS = 4
FFN_WEIGHT_CHUNKS = 4
POOL_HALO = 2 * SUBLANES
CONV_HALO = 4 * SUBLANES

F32 = jnp.float32
BF16 = jnp.bfloat16


def _rmsnorm(x, g):
    return x * lax.rsqrt(jnp.mean(x * x, axis=-1, keepdims=True) + EPS) * g


def _layernorm(x, g, b):
    mu = jnp.mean(x, axis=-1, keepdims=True)
    xc = x - mu
    var = jnp.mean(xc * xc, axis=-1, keepdims=True)
    return xc * lax.rsqrt(var + EPS) * g + b


def _gelu(x):
    return 0.5 * x * (1.0 + lax.erf(x * (2.0 ** -0.5)))


def _dot(a, b):
    return jnp.dot(a, b, preferred_element_type=F32)


def _resident(shape):
    zeros = (0,) * len(shape)
    return pl.BlockSpec(shape, lambda *_: zeros, pipeline_mode=pl.Buffered(1))


def _layer(stacked, layer):
    zeros = (0,) * (stacked.ndim - 1)
    return pl.BlockSpec((None,) + stacked.shape[1:], lambda *_: (layer,) + zeros, pipeline_mode=pl.Buffered(1))


def _params(n_grid_dims):
    return pltpu.CompilerParams(dimension_semantics=("arbitrary",) * n_grid_dims,
                                vmem_limit_bytes=VMEM_LIMIT_BYTES)


def _ffn_kernel(x_ref, g_ref, w1_hbm, w2_hbm, gf_ref, o_ref, w1_ref, w2_ref, sem, *, layer, final):
    i = pl.program_id(0)
    dff = w1_ref.shape[1]
    chunk = dff // FFN_WEIGHT_CHUNKS

    def copies(c):
        cols = pl.ds(c * chunk, chunk)
        return (pltpu.make_async_copy(w1_hbm.at[layer, :, cols], w1_ref.at[:, cols], sem.at[0, c]),
                pltpu.make_async_copy(w2_hbm.at[layer, cols, :], w2_ref.at[cols, :], sem.at[1, c]))

    def finish(x, y):
        if final:
            y = _rmsnorm(y, gf_ref[...])
        o_ref[...] = y

    @pl.when(i == 0)
    def _():
        for c in range(FFN_WEIGHT_CHUNKS):
            for cp in copies(c):
                cp.start()
        x = x_ref[...]
        h = _rmsnorm(x, g_ref[...])
        y = x
        for c in range(FFN_WEIGHT_CHUNKS):
            cp1, cp2 = copies(c)
            cp1.wait()
            a = jnp.square(jnp.maximum(_dot(h, w1_ref[:, c * chunk:(c + 1) * chunk]), 0.0))
            cp2.wait()
            y = y + _dot(a, w2_ref[c * chunk:(c + 1) * chunk, :])
        finish(x, y)

    @pl.when(i > 0)
    def _():
        x = x_ref[...]
        h = _rmsnorm(x, g_ref[...])
        a = jnp.square(jnp.maximum(_dot(h, w1_ref[...]), 0.0))
        finish(x, x + _dot(a, w2_ref[...]))


def _ffn(x, g, w1, w2, gf, *, layer, final, tm):
    rows, d = x.shape
    dff = w1.shape[2]
    assert dff % (FFN_WEIGHT_CHUNKS * LANES) == 0
    return pl.pallas_call(
        functools.partial(_ffn_kernel, layer=layer, final=final),
        grid=(rows // tm,),
        in_specs=[pl.BlockSpec((tm, d), lambda i: (i, 0)), _resident((1, d)),
                  pl.BlockSpec(memory_space=pl.ANY), pl.BlockSpec(memory_space=pl.ANY), _resident((1, d))],
        out_specs=pl.BlockSpec((tm, d), lambda i: (i, 0)),
        out_shape=jax.ShapeDtypeStruct((rows, d), F32),
        scratch_shapes=[pltpu.VMEM((d, dff), F32), pltpu.VMEM((dff, d), F32),
                        pltpu.SemaphoreType.DMA((2, FFN_WEIGHT_CHUNKS))],
        compiler_params=_params(1),
        name="ffn",
    )(x, g, w1, w2, gf)


def _ab_project(x, gn, win, lng, lnb, d_pool, d_gmlp):
    h = _rmsnorm(x, gn)
    z = _dot(h, win)
    a = z[:, :d_pool]
    u = _gelu(z[:, d_pool:d_pool + d_gmlp])
    v = _gelu(z[:, d_pool + d_gmlp:])
    return a, u, _layernorm(v, lng, lnb)


def _pool_project(pooled, pw_ref, scale):
    group = pw_ref.shape[1]
    mixed = [_dot(pooled[:, g * group:(g + 1) * group], pw_ref[g])
             for g in range(pw_ref.shape[0])]
    return jnp.concatenate(mixed, axis=1) * scale


def _ab_prompt_kernel(x_ref, gn_ref, win_ref, pw_ref, ps_ref, lng_ref, lnb_ref, ws_ref, bst_ref, wout_ref,
                      y_ref, hist_ref, aext_ref, *, tm):
    j = pl.program_id(1)
    d_pool = ps_ref.shape[1]
    d_gmlp = lng_ref.shape[1]
    n_heads = ws_ref.shape[0]
    head = d_gmlp // n_heads
    group = d_pool // len(POOL_WINDOWS)
    halo = POOL_HALO

    @pl.when(j == 0)
    def _():
        aext_ref[0:halo, :] = jnp.zeros((halo, d_pool), F32)

    @pl.when(j > 0)
    def _():
        aext_ref[0:halo, :] = aext_ref[tm:tm + halo, :]

    x = x_ref[...]
    a, u, v = _ab_project(x, gn_ref[...], win_ref[...], lng_ref[...], lnb_ref[...], d_pool, d_gmlp)
    aext_ref[halo:halo + tm, :] = a
    hist_ref[...] = a[tm - POOL_HIST:, :]

    pos = j * tm + lax.broadcasted_iota(jnp.int32, (tm, 1), 0)
    pooled = []
    for g, w in enumerate(POOL_WINDOWS):
        cols = slice(g * group, (g + 1) * group)
        s = a[:, cols]
        for k in range(1, w):
            s = s + aext_ref[halo - k:halo - k + tm, cols]
        cnt = jnp.minimum(pos + 1, w).astype(F32)
        pooled.append(s / cnt - a[:, cols])
    pool_out = _pool_project(jnp.concatenate(pooled, axis=1), pw_ref, ps_ref[...])

    n_chunks = tm // GMLP_CHUNK
    tril = (lax.broadcasted_iota(jnp.int32, (GMLP_CHUNK, GMLP_CHUNK), 0)
            >= lax.broadcasted_iota(jnp.int32, (GMLP_CHUNK, GMLP_CHUNK), 1))
    mixed = []
    for h in range(n_heads):
        wm = jnp.where(tril, ws_ref[h], 0.0)
        rhs = jnp.concatenate([v[c * GMLP_CHUNK:(c + 1) * GMLP_CHUNK, h * head:(h + 1) * head]
                               for c in range(n_chunks)], axis=1)
        mixed.append(_dot(wm, rhs) + bst_ref[:, h:h + 1])
    gate = jnp.concatenate(
        [jnp.concatenate([mixed[h][:, c * head:(c + 1) * head] for h in range(n_heads)], axis=1)
         for c in range(n_chunks)], axis=0)

    cat = jnp.concatenate([pool_out, u * gate], axis=1)
    y_ref[...] = x + _dot(cat, wout_ref[...])


def _ab_prompt(x, gn, win, pw, ps, lng, lnb, ws, bst, wout, *, layer, batch, seq, tm):
    d = x.shape[1]
    d_pool = ps.shape[1]
    assert tm % GMLP_CHUNK == 0 and seq % tm == 0
    tiles = seq // tm
    weights = (gn, win, pw, ps, lng, lnb, ws, bst, wout)
    stacked = (win, pw, ws, wout)
    return pl.pallas_call(
        functools.partial(_ab_prompt_kernel, tm=tm),
        grid=(batch, tiles),
        in_specs=[pl.BlockSpec((tm, d), lambda b, j: (b * tiles + j, 0))]
                 + [_layer(w, layer) if any(w is s for s in stacked) else _resident(w.shape) for w in weights],
        out_specs=[pl.BlockSpec((tm, d), lambda b, j: (b * tiles + j, 0)),
                   pl.BlockSpec((None, POOL_HIST, d_pool), lambda b, j: (b, 0, 0))],
        out_shape=[jax.ShapeDtypeStruct((batch * seq, d), F32),
                   jax.ShapeDtypeStruct((batch, POOL_HIST, d_pool), F32)],
        scratch_shapes=[pltpu.VMEM((POOL_HALO + tm, d_pool), F32)],
        compiler_params=_params(2),
        name="ab_prompt",
    )(x, *weights)


def _ab_sample_kernel(x_ref, hist_ref, gn_ref, win_ref, pw_ref, ps_ref, lng_ref, lnb_ref, coef_ref, bias_ref,
                      wout_ref, y_ref, nhist_ref, v_ref, *, nb, nt, past_len):
    d_pool = ps_ref.shape[1]
    d_gmlp = lng_ref.shape[1]
    group = d_pool // len(POOL_WINDOWS)

    x = x_ref[...]
    a, u, v = _ab_project(x, gn_ref[...], win_ref[...], lng_ref[...], lnb_ref[...], d_pool, d_gmlp)
    v_ref[...] = v

    def ext(k):
        return hist_ref[k] if k < POOL_HIST else a[(k - POOL_HIST) * nb:(k - POOL_HIST + 1) * nb, :]

    for k in range(POOL_HIST):
        nhist_ref[k] = ext(k + nt)

    pooled, gate = [], []
    for t in range(nt):
        groups = []
        for g, w in enumerate(POOL_WINDOWS):
            cols = slice(g * group, (g + 1) * group)
            s = ext(POOL_HIST + t)[:, cols]
            for k in range(1, w):
                s = s + ext(POOL_HIST + t - k)[:, cols]
            cnt = float(min(past_len + t + 1, w))
            groups.append(s / cnt - a[t * nb:(t + 1) * nb, cols])
        pooled.append(jnp.concatenate(groups, axis=1))
        m = bias_ref[t:t + 1, :]
        for s_ in range(t + 1):
            vs = v[s_ * nb:(s_ + 1) * nb, :].astype(BF16).astype(F32)
            m = m + coef_ref[t * nt + s_:t * nt + s_ + 1, :].astype(BF16).astype(F32) * vs
        gate.append(m)
    pool_out = _pool_project(jnp.concatenate(pooled, axis=0), pw_ref, ps_ref[...])
    cat = jnp.concatenate([pool_out, u * jnp.concatenate(gate, axis=0)], axis=1)
    y_ref[...] = x + _dot(cat, wout_ref[...])


def _ab_sample(x, hist_t, gn, win, pw, ps, lng, lnb, coef, bias, wout, *, layer, nb, nt, past_len):
    rows, d = x.shape
    d_pool = ps.shape[1]
    d_gmlp = lng.shape[1]
    operands = (x, hist_t, gn, win, pw, ps, lng, lnb, coef, bias, wout)
    stacked = (hist_t, win, pw, wout)
    return pl.pallas_call(
        functools.partial(_ab_sample_kernel, nb=nb, nt=nt, past_len=past_len),
        grid=(1,),
        in_specs=[_layer(o, layer) if any(o is s for s in stacked) else _resident(o.shape) for o in operands],
        out_specs=[_resident((rows, d)), _resident((POOL_HIST, nb, d_pool)), _resident((rows, d_gmlp))],
        out_shape=[jax.ShapeDtypeStruct((rows, d), F32),
                   jax.ShapeDtypeStruct((POOL_HIST, nb, d_pool), F32),
                   jax.ShapeDtypeStruct((rows, d_gmlp), F32)],
        compiler_params=_params(1),
        name="ab_sample",
    )(*operands)


def _glu_project(x, gn, wpw1, d_conv):
    h = _rmsnorm(x, gn)
    z = _dot(h, wpw1)
    return z[:, :d_conv] * jax.nn.sigmoid(z[:, d_conv:])


def _conv_tail(c, lng, lnb, wpw2):
    c = _layernorm(c, lng, lnb)
    return _dot(c * jax.nn.sigmoid(c), wpw2)


def _c_prompt_kernel(x_ref, gn_ref, wpw1_ref, wdw_ref, bdw_ref, lng_ref, lnb_ref, wpw2_ref,
                     y_ref, hist_ref, ext_ref, conv_ref, *, tm):
    j = pl.program_id(1)
    d_conv = bdw_ref.shape[1]
    halo = CONV_HALO
    lead = halo - CONV_HIST
    n_pairs = d_conv // (2 * LANES)

    @pl.when(j == 0)
    def _():
        ext_ref[:, 0:2 * halo, :] = jnp.zeros((n_pairs, 2 * halo, LANES), F32)

    @pl.when(j > 0)
    def _():
        ext_ref[:, 0:2 * halo, :] = ext_ref[:, 2 * tm:2 * (tm + halo), :]

    x = x_ref[...]
    h = _rmsnorm(x, gn_ref[...])
    rb = CONV_ROW_BLOCK
    width = d_conv // GLU_COLUMN_CHUNKS
    for c0 in range(0, d_conv, width):
        glu = _dot(h, wpw1_ref[:, c0:c0 + width]) * jax.nn.sigmoid(_dot(h, wpw1_ref[:, d_conv + c0:d_conv + c0 + width]))
        hist_ref[:, c0:c0 + width] = glu[tm - CONV_HIST:, :]
        blocks = range(c0 // LANES, (c0 + width) // LANES)
        for cb in blocks:
            ext_ref[cb // 2, pl.ds(2 * halo + cb % 2, tm, stride=2), :] = glu[:, cb * LANES - c0:(cb + 1) * LANES - c0]
        for cb in blocks:
            cols = slice(cb * LANES, (cb + 1) * LANES)
            w = [jnp.broadcast_to(wdw_ref[k:k + 1, cols], (rb, LANES)) for k in range(CONV_WIDTH)]
            b = jnp.broadcast_to(bdw_ref[:, cols], (rb, LANES))
            for r in range(0, tm, rb):
                chains = [b] + [None] * (CONV_CHAINS - 1)
                for k in range(CONV_WIDTH):
                    term = w[k] * ext_ref[cb // 2, pl.ds(2 * (lead + r + k) + cb % 2, rb, stride=2), :]
                    c = k % CONV_CHAINS
                    chains[c] = term if chains[c] is None else chains[c] + term
                conv_ref[r:r + rb, cols] = functools.reduce(lambda p, q: p + q, chains)

    y_ref[...] = x + _conv_tail(conv_ref[...], lng_ref[...], lnb_ref[...], wpw2_ref[...])


def _c_prompt(x, gn, wpw1, wdw, bdw, lng, lnb, wpw2, *, layer, batch, seq, tm):
    d = x.shape[1]
    d_conv = bdw.shape[1]
    assert seq % tm == 0 and tm % CONV_ROW_BLOCK == 0 and d_conv % (2 * LANES * GLU_COLUMN_CHUNKS) == 0
    tiles = seq // tm
    weights = (gn, wpw1, wdw, bdw, lng, lnb, wpw2)
    stacked = (wpw1, wpw2)
    return pl.pallas_call(
        functools.partial(_c_prompt_kernel, tm=tm),
        grid=(batch, tiles),
        in_specs=[pl.BlockSpec((tm, d), lambda b, j: (b * tiles + j, 0))]
                 + [_layer(w, layer) if any(w is s for s in stacked) else _resident(w.shape) for w in weights],
        out_specs=[pl.BlockSpec((tm, d), lambda b, j: (b * tiles + j, 0)),
                   pl.BlockSpec((None, CONV_HIST, d_conv), lambda b, j: (b, 0, 0))],
        out_shape=[jax.ShapeDtypeStruct((batch * seq, d), F32),
                   jax.ShapeDtypeStruct((batch, CONV_HIST, d_conv), F32)],
        scratch_shapes=[pltpu.VMEM((d_conv // (2 * LANES), 2 * (CONV_HALO + tm), LANES), F32),
                        pltpu.VMEM((tm, d_conv), F32)],
        compiler_params=_params(2),
        name="c_prompt",
    )(x, *weights)


def _c_sample_kernel(x_ref, hist_ref, gn_ref, wpw1_ref, wdw_ref, bdw_ref, lng_ref, lnb_ref, wpw2_ref,
                     y_ref, nhist_ref, conv_ref, *, nb, nt):
    d_conv = bdw_ref.shape[1]
    x = x_ref[...]
    glu = _glu_project(x, gn_ref[...], wpw1_ref[...], d_conv)

    def ext(k):
        return hist_ref[k] if k < CONV_HIST else glu[(k - CONV_HIST) * nb:(k - CONV_HIST + 1) * nb, :]

    for k in range(CONV_HIST):
        nhist_ref[k] = ext(k + nt)

    for t in range(nt):
        acc = jnp.broadcast_to(bdw_ref[...], (nb, d_conv))
        for k in range(CONV_WIDTH):
            acc = acc + wdw_ref[k:k + 1, :] * ext(t + k)
        conv_ref[t * nb:(t + 1) * nb, :] = acc

    y_ref[...] = x + _conv_tail(conv_ref[...], lng_ref[...], lnb_ref[...], wpw2_ref[...])


def _c_sample(x, hist_t, gn, wpw1, wdw, bdw, lng, lnb, wpw2, *, layer, nb, nt):
    rows, d = x.shape
    d_conv = bdw.shape[1]
    operands = (x, hist_t, gn, wpw1, wdw, bdw, lng, lnb, wpw2)
    stacked = (hist_t, wpw1, wpw2)
    return pl.pallas_call(
        functools.partial(_c_sample_kernel, nb=nb, nt=nt),
        grid=(1,),
        in_specs=[_layer(o, layer) if any(o is s for s in stacked) else _resident(o.shape) for o in operands],
        out_specs=[_resident((rows, d)), _resident((CONV_HIST, nb, d_conv))],
        out_shape=[jax.ShapeDtypeStruct((rows, d), F32),
                   jax.ShapeDtypeStruct((CONV_HIST, nb, d_conv), F32)],
        scratch_shapes=[pltpu.VMEM((rows, d_conv), F32)],
        compiler_params=_params(1),
        name="c_sample",
    )(*operands)


def kernel(x_prompt, x_sample, state_pool, state_conv, norm_mix, norm_ffn, norm_final, ab_w_in, ab_pool_w, ab_pool_scale, ab_ln_g, ab_ln_b, ab_ws, ab_bs, ab_w_out, c_w_pw1, c_w_dw, c_b_dw, c_ln_g, c_ln_b, c_w_pw2, ffn_w1, ffn_w2):
    batch, seq, d = x_prompt.shape
    nb, nt, _ = x_sample.shape
    depth = norm_mix.shape[0]
    past_len = PAST_LEN
    assert past_len % GMLP_CHUNK == 0 and nt <= GMLP_CHUNK

    row = lambda v: v.reshape(1, -1)
    xp = x_prompt.reshape(batch * seq, d)
    xs = jnp.transpose(x_sample, (1, 0, 2)).reshape(nt * nb, d)
    gf = row(norm_final)
    pool_t = jnp.transpose(state_pool, (0, 2, 1, 3))
    conv_t = jnp.transpose(state_conv, (0, 2, 1, 3))

    pool_p, pool_s, conv_p, conv_s, gate_v_s = [], [], [], [], []
    for layer in range(depth):
        i = layer // 2
        gn = row(norm_mix[layer])
        if layer % 2 == 0:
            ps, lng, lnb = row(ab_pool_scale[i]), row(ab_ln_g[i]), row(ab_ln_b[i])
            head = lng.shape[1] // ab_ws.shape[1]
            xp, hp = _ab_prompt(xp, gn, ab_w_in, ab_pool_w, ps, lng, lnb, ab_ws, ab_bs[i].T, ab_w_out,
                                layer=i, batch=batch, seq=seq, tm=AB_ROW_TILE)
            coef = jnp.repeat(jnp.transpose(ab_ws[i][:, :nt, :nt], (1, 2, 0)).reshape(nt * nt, -1), head, axis=1)
            bias = jnp.repeat(ab_bs[i][:, :nt].T, head, axis=1)
            xs, hs, vs = _ab_sample(xs, pool_t, gn, ab_w_in, ab_pool_w, ps, lng, lnb, coef, bias, ab_w_out,
                                    layer=i, nb=nb, nt=nt, past_len=past_len)
            pool_p.append(hp)
            pool_s.append(jnp.transpose(hs, (1, 0, 2)))
            gate_v_s.append(jnp.transpose(vs.reshape(nt, nb, -1), (1, 0, 2)))
        else:
            bdw, lng, lnb = row(c_b_dw[i]), row(c_ln_g[i]), row(c_ln_b[i])
            xp, hp = _c_prompt(xp, gn, c_w_pw1, c_w_dw[i], bdw, lng, lnb, c_w_pw2,
                               layer=i, batch=batch, seq=seq, tm=ROW_TILE)
            xs, hs = _c_sample(xs, conv_t, gn, c_w_pw1, c_w_dw[i], bdw, lng, lnb, c_w_pw2, layer=i, nb=nb, nt=nt)
            conv_p.append(hp)
            conv_s.append(jnp.transpose(hs, (1, 0, 2)))
        gff = row(norm_ffn[layer])
        final = layer == depth - 1
        xp = _ffn(xp, gff, ffn_w1, ffn_w2, gf, layer=layer, final=final, tm=ROW_TILE)
        xs = _ffn(xs, gff, ffn_w1, ffn_w2, gf, layer=layer, final=final, tm=nt * nb)

    y_prompt = xp.reshape(batch, seq, d)
    y_sample = jnp.transpose(xs.reshape(nt, nb, d), (1, 0, 2))
    return (y_prompt, y_sample, jnp.stack(pool_p), jnp.stack(pool_s), jnp.stack(conv_p), jnp.stack(conv_s),
            jnp.stack(gate_v_s))
```

```python
import functools

import jax
import jax.numpy as jnp
from jax import lax
from jax.experimental import pallas as pl
from jax.experimental.pallas import tpu as pltpu

EPS = 1e-6
POOL_WINDOWS = (2, 4, 8, 16)
POOL_HIST = max(POOL_WINDOWS) - 1
GMLP_CHUNK = 128
CONV_WIDTH = 31
CONV_HIST = CONV_WIDTH - 1
PAST_LEN = 16384

LANES = 128
SUBLANES = 8
VMEM_LIMIT_BYTES = 56 * 1024 * 1024
ROW_TILE = 512
AB_ROW_TILE = 1024
CONV_ROW_BLOCK = 16
CONV_CHAINS = 2
GLU_COLUMN_CHUNKS = 4
FFN_WEIGHT_CHUNKS = 4
POOL_HALO = 2 * SUBLANES
CONV_HALO = 4 * SUBLANES

F32 = jnp.float32
BF16 = jnp.bfloat16


def _rmsnorm(x, g):
    return x * lax.rsqrt(jnp.mean(x * x, axis=-1, keepdims=True) + EPS) * g


def _layernorm(x, g, b):
    mu = jnp.mean(x, axis=-1, keepdims=True)
    xc = x - mu
    var = jnp.mean(xc * xc, axis=-1, keepdims=True)
    return xc * lax.rsqrt(var + EPS) * g + b


def _gelu(x):
    return 0.5 * x * (1.0 + lax.erf(x * (2.0 ** -0.5)))


def _dot(a, b):
    return jnp.dot(a, b, preferred_element_type=F32)


def _resident(shape):
    zeros = (0,) * len(shape)
    return pl.BlockSpec(shape, lambda *_: zeros, pipeline_mode=pl.Buffered(1))


def _layer(stacked, layer):
    zeros = (0,) * (stacked.ndim - 1)
    return pl.BlockSpec((None,) + stacked.shape[1:], lambda *_: (layer,) + zeros, pipeline_mode=pl.Buffered(1))


def _params(n_grid_dims):
    return pltpu.CompilerParams(dimension_semantics=("arbitrary",) * n_grid_dims,
                                vmem_limit_bytes=VMEM_LIMIT_BYTES)


def _ffn_kernel(x_ref, g_ref, w1_hbm, w2_hbm, gf_ref, o_ref, w1_ref, w2_ref, sem, *, layer, final):
    i = pl.program_id(0)
    dff = w1_ref.shape[1]
    chunk = dff // FFN_WEIGHT_CHUNKS

    def copies(c):
        cols = pl.ds(c * chunk, chunk)
        return (pltpu.make_async_copy(w1_hbm.at[layer, :, cols], w1_ref.at[:, cols], sem.at[0, c]),
                pltpu.make_async_copy(w2_hbm.at[layer, cols, :], w2_ref.at[cols, :], sem.at[1, c]))

    def finish(x, y):
        if final:
            y = _rmsnorm(y, gf_ref[...])
        o_ref[...] = y

    @pl.when(i == 0)
    def _():
        for c in range(FFN_WEIGHT_CHUNKS):
            for cp in copies(c):
                cp.start()
        x = x_ref[...]
        h = _rmsnorm(x, g_ref[...])
        y = x
        for c in range(FFN_WEIGHT_CHUNKS):
            cp1, cp2 = copies(c)
            cp1.wait()
            a = jnp.square(jnp.maximum(_dot(h, w1_ref[:, c * chunk:(c + 1) * chunk]), 0.0))
            cp2.wait()
            y = y + _dot(a, w2_ref[c * chunk:(c + 1) * chunk, :])
        finish(x, y)

    @pl.when(i > 0)
    def _():
        x = x_ref[...]
        h = _rmsnorm(x, g_ref[...])
        a = jnp.square(jnp.maximum(_dot(h, w1_ref[...]), 0.0))
        finish(x, x + _dot(a, w2_ref[...]))


def _ffn(x, g, w1, w2, gf, *, layer, final, tm):
    rows, d = x.shape
    dff = w1.shape[2]
    assert dff % (FFN_WEIGHT_CHUNKS * LANES) == 0
    return pl.pallas_call(
        functools.partial(_ffn_kernel, layer=layer, final=final),
        grid=(rows // tm,),
        in_specs=[pl.BlockSpec((tm, d), lambda i: (i, 0)), _resident((1, d)),
                  pl.BlockSpec(memory_space=pl.ANY), pl.BlockSpec(memory_space=pl.ANY), _resident((1, d))],
        out_specs=pl.BlockSpec((tm, d), lambda i: (i, 0)),
        out_shape=jax.ShapeDtypeStruct((rows, d), F32),
        scratch_shapes=[pltpu.VMEM((d, dff), F32), pltpu.VMEM((dff, d), F32),
                        pltpu.SemaphoreType.DMA((2, FFN_WEIGHT_CHUNKS))],
        compiler_params=_params(1),
        name="ffn",
    )(x, g, w1, w2, gf)


def _ab_project(x, gn, win, lng, lnb, d_pool, d_gmlp):
    h = _rmsnorm(x, gn)
    z = _dot(h, win)
    a = z[:, :d_pool]
    u = _gelu(z[:, d_pool:d_pool + d_gmlp])
    v = _gelu(z[:, d_pool + d_gmlp:])
    return a, u, _layernorm(v, lng, lnb)


def _pool_project(pooled, pw_ref, scale):
    group = pw_ref.shape[1]
    mixed = [_dot(pooled[:, g * group:(g + 1) * group], pw_ref[g])
             for g in range(pw_ref.shape[0])]
    return jnp.concatenate(mixed, axis=1) * scale


def _ab_prompt_kernel(x_ref, gn_ref, win_ref, pw_ref, ps_ref, lng_ref, lnb_ref, ws_ref, bst_ref, wout_ref,
                      y_ref, hist_ref, aext_ref, *, tm):
    j = pl.program_id(1)
    d_pool = ps_ref.shape[1]
    d_gmlp = lng_ref.shape[1]
    n_heads = ws_ref.shape[0]
    head = d_gmlp // n_heads
    group = d_pool // len(POOL_WINDOWS)
    halo = POOL_HALO

    @pl.when(j == 0)
    def _():
        aext_ref[0:halo, :] = jnp.zeros((halo, d_pool), F32)

    @pl.when(j > 0)
    def _():
        aext_ref[0:halo, :] = aext_ref[tm:tm + halo, :]

    x = x_ref[...]
    a, u, v = _ab_project(x, gn_ref[...], win_ref[...], lng_ref[...], lnb_ref[...], d_pool, d_gmlp)
    aext_ref[halo:halo + tm, :] = a
    hist_ref[...] = a[tm - POOL_HIST:, :]

    pos = j * tm + lax.broadcasted_iota(jnp.int32, (tm, 1), 0)
    pooled = []
    for g, w in enumerate(POOL_WINDOWS):
        cols = slice(g * group, (g + 1) * group)
        s = a[:, cols]
        for k in range(1, w):
            s = s + aext_ref[halo - k:halo - k + tm, cols]
        cnt = jnp.minimum(pos + 1, w).astype(F32)
        pooled.append(s / cnt - a[:, cols])
    pool_out = _pool_project(jnp.concatenate(pooled, axis=1), pw_ref, ps_ref[...])

    n_chunks = tm // GMLP_CHUNK
    tril = (lax.broadcasted_iota(jnp.int32, (GMLP_CHUNK, GMLP_CHUNK), 0)
            >= lax.broadcasted_iota(jnp.int32, (GMLP_CHUNK, GMLP_CHUNK), 1))
    mixed = []
    for h in range(n_heads):
        wm = jnp.where(tril, ws_ref[h], 0.0)
        rhs = jnp.concatenate([v[c * GMLP_CHUNK:(c + 1) * GMLP_CHUNK, h * head:(h + 1) * head]
                               for c in range(n_chunks)], axis=1)
        mixed.append(_dot(wm, rhs) + bst_ref[:, h:h + 1])
    gate = jnp.concatenate(
        [jnp.concatenate([mixed[h][:, c * head:(c + 1) * head] for h in range(n_heads)], axis=1)
         for c in range(n_chunks)], axis=0)

    cat = jnp.concatenate([pool_out, u * gate], axis=1)
    y_ref[...] = x + _dot(cat, wout_ref[...])


def _ab_prompt(x, gn, win, pw, ps, lng, lnb, ws, bst, wout, *, layer, batch, seq, tm):
    d = x.shape[1]
    d_pool = ps.shape[1]
    assert tm % GMLP_CHUNK == 0 and seq % tm == 0
    tiles = seq // tm
    weights = (gn, win, pw, ps, lng, lnb, ws, bst, wout)
    stacked = (win, pw, ws, wout)
    return pl.pallas_call(
        functools.partial(_ab_prompt_kernel, tm=tm),
        grid=(batch, tiles),
        in_specs=[pl.BlockSpec((tm, d), lambda b, j: (b * tiles + j, 0))]
                 + [_layer(w, layer) if any(w is s for s in stacked) else _resident(w.shape) for w in weights],
        out_specs=[pl.BlockSpec((tm, d), lambda b, j: (b * tiles + j, 0)),
                   pl.BlockSpec((None, POOL_HIST, d_pool), lambda b, j: (b, 0, 0))],
        out_shape=[jax.ShapeDtypeStruct((batch * seq, d), F32),
                   jax.ShapeDtypeStruct((batch, POOL_HIST, d_pool), F32)],
        scratch_shapes=[pltpu.VMEM((POOL_HALO + tm, d_pool), F32)],
        compiler_params=_params(2),
        name="ab_prompt",
    )(x, *weights)


def _ab_sample_kernel(x_ref, hist_hbm, gn_ref, win_hbm, pw_ref, ps_ref, lng_ref, lnb_ref, coef_ref, bias_ref,
                      wout_hbm, y_ref, nhist_hbm, v_ref, win_ref, wout_ref, hist_ref, a_ref, sem,
                      *, layer, nb, nt, past_len):
    d_pool = ps_ref.shape[1]
    d_gmlp = lng_ref.shape[1]
    group = d_pool // len(POOL_WINDOWS)
    load_win = pltpu.make_async_copy(win_hbm.at[layer], win_ref, sem.at[0])
    load_hist = pltpu.make_async_copy(hist_hbm.at[layer], hist_ref, sem.at[1])
    load_wout = pltpu.make_async_copy(wout_hbm.at[layer], wout_ref, sem.at[2])
    shift_hist = pltpu.make_async_copy(hist_hbm.at[layer, pl.ds(nt, POOL_HIST - nt)],
                                       nhist_hbm.at[pl.ds(0, POOL_HIST - nt)], sem.at[3])
    store_new = pltpu.make_async_copy(a_ref, nhist_hbm.at[pl.ds(POOL_HIST - nt, nt)], sem.at[4])
    for cp in (load_win, load_hist, load_wout, shift_hist):
        cp.start()

    x = x_ref[...]
    load_win.wait()
    a, u, v = _ab_project(x, gn_ref[...], win_ref[...], lng_ref[...], lnb_ref[...], d_pool, d_gmlp)
    v_ref[...] = v
    for t in range(nt):
        a_ref[t] = a[t * nb:(t + 1) * nb, :]
    store_new.start()

    def ext(k):
        return hist_ref[k] if k < POOL_HIST else a_ref[k - POOL_HIST]

    load_hist.wait()
    pooled, gate = [], []
    for t in range(nt):
        groups = []
        for g, w in enumerate(POOL_WINDOWS):
            cols = slice(g * group, (g + 1) * group)
            s = ext(POOL_HIST + t)[:, cols]
            for k in range(1, w):
                s = s + ext(POOL_HIST + t - k)[:, cols]
            cnt = float(min(past_len + t + 1, w))
            groups.append(s / cnt - a[t * nb:(t + 1) * nb, cols])
        pooled.append(jnp.concatenate(groups, axis=1))
        m = bias_ref[t:t + 1, :]
        for s_ in range(t + 1):
            vs = v[s_ * nb:(s_ + 1) * nb, :].astype(BF16).astype(F32)
            m = m + coef_ref[t * nt + s_:t * nt + s_ + 1, :].astype(BF16).astype(F32) * vs
        gate.append(m)
    pool_out = _pool_project(jnp.concatenate(pooled, axis=0), pw_ref, ps_ref[...])
    cat = jnp.concatenate([pool_out, u * jnp.concatenate(gate, axis=0)], axis=1)
    load_wout.wait()
    y_ref[...] = x + _dot(cat, wout_ref[...])
    shift_hist.wait()
    store_new.wait()


def _ab_sample(x, hist_t, gn, win, pw, ps, lng, lnb, coef, bias, wout, *, layer, nb, nt, past_len):
    rows, d = x.shape
    d_pool = ps.shape[1]
    d_gmlp = lng.shape[1]
    any_spec = pl.BlockSpec(memory_space=pl.ANY)
    small = lambda w: _resident(w.shape)
    return pl.pallas_call(
        functools.partial(_ab_sample_kernel, layer=layer, nb=nb, nt=nt, past_len=past_len),
        grid=(1,),
        in_specs=[_resident((rows, d)), any_spec, small(gn), any_spec, _layer(pw, layer), small(ps), small(lng),
                  small(lnb), small(coef), small(bias), any_spec],
        out_specs=[_resident((rows, d)), any_spec, _resident((rows, d_gmlp))],
        out_shape=[jax.ShapeDtypeStruct((rows, d), F32),
                   jax.ShapeDtypeStruct((POOL_HIST, nb, d_pool), F32),
                   jax.ShapeDtypeStruct((rows, d_gmlp), F32)],
        scratch_shapes=[pltpu.VMEM(win.shape[1:], F32), pltpu.VMEM(wout.shape[1:], F32),
                        pltpu.VMEM((POOL_HIST, nb, d_pool), F32), pltpu.VMEM((nt, nb, d_pool), F32),
                        pltpu.SemaphoreType.DMA((5,))],
        compiler_params=_params(1),
        name="ab_sample",
    )(x, hist_t, gn, win, pw, ps, lng, lnb, coef, bias, wout)


def _glu_project(x, gn, wpw1, d_conv):
    h = _rmsnorm(x, gn)
    z = _dot(h, wpw1)
    return z[:, :d_conv] * jax.nn.sigmoid(z[:, d_conv:])


def _conv_tail(c, lng, lnb, wpw2):
    c = _layernorm(c, lng, lnb)
    return _dot(c * jax.nn.sigmoid(c), wpw2)


def _c_prompt_kernel(x_ref, gn_ref, wpw1_ref, wdw_ref, bdw_ref, lng_ref, lnb_ref, wpw2_ref,
                     y_ref, hist_ref, ext_ref, conv_ref, *, tm):
    j = pl.program_id(1)
    d_conv = bdw_ref.shape[1]
    halo = CONV_HALO
    lead = halo - CONV_HIST
    n_pairs = d_conv // (2 * LANES)

    @pl.when(j == 0)
    def _():
        ext_ref[:, 0:2 * halo, :] = jnp.zeros((n_pairs, 2 * halo, LANES), F32)

    @pl.when(j > 0)
    def _():
        ext_ref[:, 0:2 * halo, :] = ext_ref[:, 2 * tm:2 * (tm + halo), :]

    x = x_ref[...]
    h = _rmsnorm(x, gn_ref[...])
    rb = CONV_ROW_BLOCK
    width = d_conv // GLU_COLUMN_CHUNKS
    for c0 in range(0, d_conv, width):
        glu = _dot(h, wpw1_ref[:, c0:c0 + width]) * jax.nn.sigmoid(_dot(h, wpw1_ref[:, d_conv + c0:d_conv + c0 + width]))
        hist_ref[:, c0:c0 + width] = glu[tm - CONV_HIST:, :]
        blocks = range(c0 // LANES, (c0 + width) // LANES)
        for cb in blocks:
            ext_ref[cb // 2, pl.ds(2 * halo + cb % 2, tm, stride=2), :] = glu[:, cb * LANES - c0:(cb + 1) * LANES - c0]
        for cb in blocks:
            cols = slice(cb * LANES, (cb + 1) * LANES)
            w = [jnp.broadcast_to(wdw_ref[k:k + 1, cols], (rb, LANES)) for k in range(CONV_WIDTH)]
            b = jnp.broadcast_to(bdw_ref[:, cols], (rb, LANES))
            for r in range(0, tm, rb):
                chains = [b] + [None] * (CONV_CHAINS - 1)
                for k in range(CONV_WIDTH):
                    term = w[k] * ext_ref[cb // 2, pl.ds(2 * (lead + r + k) + cb % 2, rb, stride=2), :]
                    c = k % CONV_CHAINS
                    chains[c] = term if chains[c] is None else chains[c] + term
                conv_ref[r:r + rb, cols] = functools.reduce(lambda p, q: p + q, chains)

    y_ref[...] = x + _conv_tail(conv_ref[...], lng_ref[...], lnb_ref[...], wpw2_ref[...])


def _c_prompt(x, gn, wpw1, wdw, bdw, lng, lnb, wpw2, *, layer, batch, seq, tm):
    d = x.shape[1]
    d_conv = bdw.shape[1]
    assert seq % tm == 0 and tm % CONV_ROW_BLOCK == 0 and d_conv % (2 * LANES * GLU_COLUMN_CHUNKS) == 0
    tiles = seq // tm
    weights = (gn, wpw1, wdw, bdw, lng, lnb, wpw2)
    stacked = (wpw1, wpw2)
    return pl.pallas_call(
        functools.partial(_c_prompt_kernel, tm=tm),
        grid=(batch, tiles),
        in_specs=[pl.BlockSpec((tm, d), lambda b, j: (b * tiles + j, 0))]
                 + [_layer(w, layer) if any(w is s for s in stacked) else _resident(w.shape) for w in weights],
        out_specs=[pl.BlockSpec((tm, d), lambda b, j: (b * tiles + j, 0)),
                   pl.BlockSpec((None, CONV_HIST, d_conv), lambda b, j: (b, 0, 0))],
        out_shape=[jax.ShapeDtypeStruct((batch * seq, d), F32),
                   jax.ShapeDtypeStruct((batch, CONV_HIST, d_conv), F32)],
        scratch_shapes=[pltpu.VMEM((d_conv // (2 * LANES), 2 * (CONV_HALO + tm), LANES), F32),
                        pltpu.VMEM((tm, d_conv), F32)],
        compiler_params=_params(2),
        name="c_prompt",
    )(x, *weights)


def _c_sample_kernel(x_ref, hist_hbm, gn_ref, wpw1_hbm, wdw_ref, bdw_ref, lng_ref, lnb_ref, wpw2_hbm,
                     y_ref, nhist_hbm, wpw1_ref, wpw2_ref, hist_ref, glu_ref, conv_ref, sem, *, layer, nb, nt):
    d_conv = bdw_ref.shape[1]
    load_w1 = pltpu.make_async_copy(wpw1_hbm.at[layer], wpw1_ref, sem.at[0])
    load_hist = pltpu.make_async_copy(hist_hbm.at[layer], hist_ref, sem.at[1])
    load_w2 = pltpu.make_async_copy(wpw2_hbm.at[layer], wpw2_ref, sem.at[2])
    shift_hist = pltpu.make_async_copy(hist_hbm.at[layer, pl.ds(nt, CONV_HIST - nt)],
                                       nhist_hbm.at[pl.ds(0, CONV_HIST - nt)], sem.at[3])
    store_new = pltpu.make_async_copy(glu_ref, nhist_hbm.at[pl.ds(CONV_HIST - nt, nt)], sem.at[4])
    for cp in (load_w1, load_hist, load_w2, shift_hist):
        cp.start()

    x = x_ref[...]
    load_w1.wait()
    glu = _glu_project(x, gn_ref[...], wpw1_ref[...], d_conv)
    for t in range(nt):
        glu_ref[t] = glu[t * nb:(t + 1) * nb, :]
    store_new.start()

    def ext(k):
        return hist_ref[k] if k < CONV_HIST else glu_ref[k - CONV_HIST]

    load_hist.wait()
    for t in range(nt):
        acc = jnp.broadcast_to(bdw_ref[...], (nb, d_conv))
        for k in range(CONV_WIDTH):
            acc = acc + wdw_ref[k:k + 1, :] * ext(t + k)
        conv_ref[t * nb:(t + 1) * nb, :] = acc

    load_w2.wait()
    y_ref[...] = x + _conv_tail(conv_ref[...], lng_ref[...], lnb_ref[...], wpw2_ref[...])
    shift_hist.wait()
    store_new.wait()


def _c_sample(x, hist_t, gn, wpw1, wdw, bdw, lng, lnb, wpw2, *, layer, nb, nt):
    rows, d = x.shape
    d_conv = bdw.shape[1]
    any_spec = pl.BlockSpec(memory_space=pl.ANY)
    return pl.pallas_call(
        functools.partial(_c_sample_kernel, layer=layer, nb=nb, nt=nt),
        grid=(1,),
        in_specs=[_resident((rows, d)), any_spec, _resident(gn.shape), any_spec, _resident(wdw.shape),
                  _resident(bdw.shape), _resident(lng.shape), _resident(lnb.shape), any_spec],
        out_specs=[_resident((rows, d)), any_spec],
        out_shape=[jax.ShapeDtypeStruct((rows, d), F32),
                   jax.ShapeDtypeStruct((CONV_HIST, nb, d_conv), F32)],
        scratch_shapes=[pltpu.VMEM(wpw1.shape[1:], F32), pltpu.VMEM(wpw2.shape[1:], F32),
                        pltpu.VMEM((CONV_HIST, nb, d_conv), F32), pltpu.VMEM((nt, nb, d_conv), F32),
                        pltpu.VMEM((rows, d_conv), F32), pltpu.SemaphoreType.DMA((5,))],
        compiler_params=_params(1),
        name="c_sample",
    )(x, hist_t, gn, wpw1, wdw, bdw, lng, lnb, wpw2)


def kernel(x_prompt, x_sample, state_pool, state_conv, norm_mix, norm_ffn, norm_final, ab_w_in, ab_pool_w, ab_pool_scale, ab_ln_g, ab_ln_b, ab_ws, ab_bs, ab_w_out, c_w_pw1, c_w_dw, c_b_dw, c_ln_g, c_ln_b, c_w_pw2, ffn_w1, ffn_w2):
    batch, seq, d = x_prompt.shape
    nb, nt, _ = x_sample.shape
    depth = norm_mix.shape[0]
    past_len = PAST_LEN
    assert past_len % GMLP_CHUNK == 0 and nt <= GMLP_CHUNK

    row = lambda v: v.reshape(1, -1)
    xp = x_prompt.reshape(batch * seq, d)
    xs = jnp.transpose(x_sample, (1, 0, 2)).reshape(nt * nb, d)
    gf = row(norm_final)
    pool_t = jnp.transpose(state_pool, (0, 2, 1, 3))
    conv_t = jnp.transpose(state_conv, (0, 2, 1, 3))

    pool_p, pool_s, conv_p, conv_s, gate_v_s = [], [], [], [], []
    for layer in range(depth):
        i = layer // 2
        gn = row(norm_mix[layer])
        if layer % 2 == 0:
            ps, lng, lnb = row(ab_pool_scale[i]), row(ab_ln_g[i]), row(ab_ln_b[i])
            head = lng.shape[1] // ab_ws.shape[1]
            xp, hp = _ab_prompt(xp, gn, ab_w_in, ab_pool_w, ps, lng, lnb, ab_ws, ab_bs[i].T, ab_w_out,
                                layer=i, batch=batch, seq=seq, tm=AB_ROW_TILE)
            coef = jnp.repeat(jnp.transpose(ab_ws[i][:, :nt, :nt], (1, 2, 0)).reshape(nt * nt, -1), head, axis=1)
            bias = jnp.repeat(ab_bs[i][:, :nt].T, head, axis=1)
            xs, hs, vs = _ab_sample(xs, pool_t, gn, ab_w_in, ab_pool_w, ps, lng, lnb, coef, bias, ab_w_out,
                                    layer=i, nb=nb, nt=nt, past_len=past_len)
            pool_p.append(hp)
            pool_s.append(jnp.transpose(hs, (1, 0, 2)))
            gate_v_s.append(jnp.transpose(vs.reshape(nt, nb, -1), (1, 0, 2)))
        else:
            bdw, lng, lnb = row(c_b_dw[i]), row(c_ln_g[i]), row(c_ln_b[i])
            xp, hp = _c_prompt(xp, gn, c_w_pw1, c_w_dw[i], bdw, lng, lnb, c_w_pw2,
                               layer=i, batch=batch, seq=seq, tm=ROW_TILE)
            xs, hs = _c_sample(xs, conv_t, gn, c_w_pw1, c_w_dw[i], bdw, lng, lnb, c_w_pw2, layer=i, nb=nb, nt=nt)
            conv_p.append(hp)
            conv_s.append(jnp.transpose(hs, (1, 0, 2)))
        gff = row(norm_ffn[layer])
        final = layer == depth - 1
        xp = _ffn(xp, gff, ffn_w1, ffn_w2, gf, layer=layer, final=final, tm=ROW_TILE)
        xs = _ffn(xs, gff, ffn_w1, ffn_w2, gf, layer=layer, final=final, tm=nt * nb)

    y_prompt = xp.reshape(batch, seq, d)
    y_sample = jnp.transpose(xs.reshape(nt, nb, d), (1, 0, 2))
    return (y_prompt, y_sample, jnp.stack(pool_p), jnp.stack(pool_s), jnp.stack(conv_p), jnp.stack(conv_s),
            jnp.stack(gate_v_s))
```

```python
import functools

import jax
import jax.numpy as jnp
from jax import lax
from jax.experimental import pallas as pl
from jax.experimental.pallas import tpu as pltpu

EPS = 1e-6
POOL_WINDOWS = (2, 4, 8, 16)
POOL_HIST = max(POOL_WINDOWS) - 1
GMLP_CHUNK = 128
CONV_WIDTH = 31
CONV_HIST = CONV_WIDTH - 1
PAST_LEN = 16384

LANES = 128
SUBLANES = 8
VMEM_LIMIT_BYTES = 56 * 1024 * 1024
ROW_TILE = 512
AB_ROW_TILE = 1024
CONV_ROW_BLOCK = 16
CONV_CHAINS = 2
GLU_COLUMN_CHUNKS = 4
FFN_WEIGHT_CHUNKS = 4
POOL_HALO = 2 * SUBLANES
CONV_HALO = 4 * SUBLANES

F32 = jnp.float32
BF16 = jnp.bfloat16


def _rmsnorm(x, g):
    return x * lax.rsqrt(jnp.mean(x * x, axis=-1, keepdims=True) + EPS) * g


def _layernorm(x, g, b):
    mu = jnp.mean(x, axis=-1, keepdims=True)
    xc = x - mu
    var = jnp.mean(xc * xc, axis=-1, keepdims=True)
    return xc * lax.rsqrt(var + EPS) * g + b


def _gelu(x):
    return 0.5 * x * (1.0 + lax.erf(x * (2.0 ** -0.5)))


def _dot(a, b):
    return jnp.dot(a, b, preferred_element_type=F32)


def _resident(shape):
    zeros = (0,) * len(shape)
    return pl.BlockSpec(shape, lambda *_: zeros, pipeline_mode=pl.Buffered(1))


def _layer(stacked, layer):
    zeros = (0,) * (stacked.ndim - 1)
    return pl.BlockSpec((None,) + stacked.shape[1:], lambda *_: (layer,) + zeros, pipeline_mode=pl.Buffered(1))


def _params(n_grid_dims):
    return pltpu.CompilerParams(dimension_semantics=("arbitrary",) * n_grid_dims,
                                vmem_limit_bytes=VMEM_LIMIT_BYTES)


def _ffn_kernel(x_ref, g_ref, w1_hbm, w2_hbm, gf_ref, o_ref, w1_ref, w2_ref, sem, *, layer, final):
    i = pl.program_id(0)
    dff = w1_ref.shape[1]
    chunk = dff // FFN_WEIGHT_CHUNKS

    def copies(c):
        cols = pl.ds(c * chunk, chunk)
        return (pltpu.make_async_copy(w1_hbm.at[layer, :, cols], w1_ref.at[:, cols], sem.at[0, c]),
                pltpu.make_async_copy(w2_hbm.at[layer, cols, :], w2_ref.at[cols, :], sem.at[1, c]))

    def finish(x, y):
        if final:
            y = _rmsnorm(y, gf_ref[...])
        o_ref[...] = y

    @pl.when(i == 0)
    def _():
        for c in range(FFN_WEIGHT_CHUNKS):
            for cp in copies(c):
                cp.start()
        x = x_ref[...]
        h = _rmsnorm(x, g_ref[...])
        y = x
        for c in range(FFN_WEIGHT_CHUNKS):
            cp1, cp2 = copies(c)
            cp1.wait()
            a = jnp.square(jnp.maximum(_dot(h, w1_ref[:, c * chunk:(c + 1) * chunk]), 0.0))
            cp2.wait()
            y = y + _dot(a, w2_ref[c * chunk:(c + 1) * chunk, :])
        finish(x, y)

    @pl.when(i > 0)
    def _():
        x = x_ref[...]
        h = _rmsnorm(x, g_ref[...])
        a = jnp.square(jnp.maximum(_dot(h, w1_ref[...]), 0.0))
        finish(x, x + _dot(a, w2_ref[...]))


def _ffn(x, g, w1, w2, gf, *, layer, final, tm):
    rows, d = x.shape
    dff = w1.shape[2]
    assert dff % (FFN_WEIGHT_CHUNKS * LANES) == 0
    return pl.pallas_call(
        functools.partial(_ffn_kernel, layer=layer, final=final),
        grid=(rows // tm,),
        in_specs=[pl.BlockSpec((tm, d), lambda i: (i, 0)), _resident((1, d)),
                  pl.BlockSpec(memory_space=pl.ANY), pl.BlockSpec(memory_space=pl.ANY), _resident((1, d))],
        out_specs=pl.BlockSpec((tm, d), lambda i: (i, 0)),
        out_shape=jax.ShapeDtypeStruct((rows, d), F32),
        scratch_shapes=[pltpu.VMEM((d, dff), F32), pltpu.VMEM((dff, d), F32),
                        pltpu.SemaphoreType.DMA((2, FFN_WEIGHT_CHUNKS))],
        compiler_params=_params(1),
        name="ffn",
    )(x, g, w1, w2, gf)


def _ab_project(x, gn, win, lng, lnb, d_pool, d_gmlp):
    h = _rmsnorm(x, gn)
    z = _dot(h, win)
    a = z[:, :d_pool]
    u = _gelu(z[:, d_pool:d_pool + d_gmlp])
    v = _gelu(z[:, d_pool + d_gmlp:])
    return a, u, _layernorm(v, lng, lnb)


def _pool_project(pooled, pw_ref, scale):
    group = pw_ref.shape[1]
    mixed = [_dot(pooled[:, g * group:(g + 1) * group], pw_ref[g])
             for g in range(pw_ref.shape[0])]
    return jnp.concatenate(mixed, axis=1) * scale


def _ab_prompt_kernel(x_ref, gn_ref, win_ref, pw_ref, ps_ref, lng_ref, lnb_ref, ws_ref, bst_ref, wout_ref,
                      y_ref, hist_ref, aext_ref, *, tm):
    j = pl.program_id(1)
    d_pool = ps_ref.shape[1]
    d_gmlp = lng_ref.shape[1]
    n_heads = ws_ref.shape[0]
    head = d_gmlp // n_heads
    group = d_pool // len(POOL_WINDOWS)
    halo = POOL_HALO

    @pl.when(j == 0)
    def _():
        aext_ref[0:halo, :] = jnp.zeros((halo, d_pool), F32)

    @pl.when(j > 0)
    def _():
        aext_ref[0:halo, :] = aext_ref[tm:tm + halo, :]

    x = x_ref[...]
    a, u, v = _ab_project(x, gn_ref[...], win_ref[...], lng_ref[...], lnb_ref[...], d_pool, d_gmlp)
    aext_ref[halo:halo + tm, :] = a
    hist_ref[...] = a[tm - POOL_HIST:, :]

    pos = j * tm + lax.broadcasted_iota(jnp.int32, (tm, 1), 0)
    pooled = []
    for g, w in enumerate(POOL_WINDOWS):
        cols = slice(g * group, (g + 1) * group)
        s = a[:, cols]
        for k in range(1, w):
            s = s + aext_ref[halo - k:halo - k + tm, cols]
        cnt = jnp.minimum(pos + 1, w).astype(F32)
        pooled.append(s / cnt - a[:, cols])
    pool_out = _pool_project(jnp.concatenate(pooled, axis=1), pw_ref, ps_ref[...])

    n_chunks = tm // GMLP_CHUNK
    tril = (lax.broadcasted_iota(jnp.int32, (GMLP_CHUNK, GMLP_CHUNK), 0)
            >= lax.broadcasted_iota(jnp.int32, (GMLP_CHUNK, GMLP_CHUNK), 1))
    mixed = []
    for h in range(n_heads):
        wm = jnp.where(tril, ws_ref[h], 0.0)
        rhs = jnp.concatenate([v[c * GMLP_CHUNK:(c + 1) * GMLP_CHUNK, h * head:(h + 1) * head]
                               for c in range(n_chunks)], axis=1)
        mixed.append(_dot(wm, rhs) + bst_ref[:, h:h + 1])
    gate = jnp.concatenate(
        [jnp.concatenate([mixed[h][:, c * head:(c + 1) * head] for h in range(n_heads)], axis=1)
         for c in range(n_chunks)], axis=0)

    cat = jnp.concatenate([pool_out, u * gate], axis=1)
    y_ref[...] = x + _dot(cat, wout_ref[...])


def _ab_prompt(x, gn, win, pw, ps, lng, lnb, ws, bst, wout, *, layer, batch, seq, tm):
    d = x.shape[1]
    d_pool = ps.shape[1]
    assert tm % GMLP_CHUNK == 0 and seq % tm == 0
    tiles = seq // tm
    weights = (gn, win, pw, ps, lng, lnb, ws, bst, wout)
    stacked = (win, pw, ws, wout)
    return pl.pallas_call(
        functools.partial(_ab_prompt_kernel, tm=tm),
        grid=(batch, tiles),
        in_specs=[pl.BlockSpec((tm, d), lambda b, j: (b * tiles + j, 0))]
                 + [_layer(w, layer) if any(w is s for s in stacked) else _resident(w.shape) for w in weights],
        out_specs=[pl.BlockSpec((tm, d), lambda b, j: (b * tiles + j, 0)),
                   pl.BlockSpec((None, POOL_HIST, d_pool), lambda b, j: (b, 0, 0))],
        out_shape=[jax.ShapeDtypeStruct((batch * seq, d), F32),
                   jax.ShapeDtypeStruct((batch, POOL_HIST, d_pool), F32)],
        scratch_shapes=[pltpu.VMEM((POOL_HALO + tm, d_pool), F32)],
        compiler_params=_params(2),
        name="ab_prompt",
    )(x, *weights)


def _ab_sample_kernel(x_ref, hist_hbm, gn_ref, win_hbm, pw_ref, ps_ref, lng_ref, lnb_ref, coef_ref, bias_ref,
                      wout_hbm, y_ref, nhist_hbm, v_ref, win_ref, wout_ref, hist_ref, a_ref, sem,
                      *, layer, nb, nt, past_len):
    d_pool = ps_ref.shape[1]
    d_gmlp = lng_ref.shape[1]
    group = d_pool // len(POOL_WINDOWS)
    load_win = pltpu.make_async_copy(win_hbm.at[layer], win_ref, sem.at[0])
    load_hist = pltpu.make_async_copy(hist_hbm.at[layer], hist_ref, sem.at[1])
    load_wout = pltpu.make_async_copy(wout_hbm.at[layer], wout_ref, sem.at[2])
    shift_hist = pltpu.make_async_copy(hist_ref.at[pl.ds(nt, POOL_HIST - nt)],
                                       nhist_hbm.at[pl.ds(0, POOL_HIST - nt)], sem.at[3])
    store_new = pltpu.make_async_copy(a_ref, nhist_hbm.at[pl.ds(POOL_HIST - nt, nt)], sem.at[4])
    for cp in (load_win, load_hist, load_wout):
        cp.start()

    x = x_ref[...]
    load_win.wait()
    a, u, v = _ab_project(x, gn_ref[...], win_ref[...], lng_ref[...], lnb_ref[...], d_pool, d_gmlp)
    v_ref[...] = v
    for t in range(nt):
        a_ref[t] = a[t * nb:(t + 1) * nb, :]
    store_new.start()

    def ext(k):
        return hist_ref[k] if k < POOL_HIST else a_ref[k - POOL_HIST]

    load_hist.wait()
    shift_hist.start()
    pooled, gate = [], []
    for t in range(nt):
        groups = []
        for g, w in enumerate(POOL_WINDOWS):
            cols = slice(g * group, (g + 1) * group)
            s = ext(POOL_HIST + t)[:, cols]
            for k in range(1, w):
                s = s + ext(POOL_HIST + t - k)[:, cols]
            cnt = float(min(past_len + t + 1, w))
            groups.append(s / cnt - a[t * nb:(t + 1) * nb, cols])
        pooled.append(jnp.concatenate(groups, axis=1))
        m = bias_ref[t:t + 1, :]
        for s_ in range(t + 1):
            vs = v[s_ * nb:(s_ + 1) * nb, :].astype(BF16).astype(F32)
            m = m + coef_ref[t * nt + s_:t * nt + s_ + 1, :].astype(BF16).astype(F32) * vs
        gate.append(m)
    pool_out = _pool_project(jnp.concatenate(pooled, axis=0), pw_ref, ps_ref[...])
    cat = jnp.concatenate([pool_out, u * jnp.concatenate(gate, axis=0)], axis=1)
    load_wout.wait()
    y_ref[...] = x + _dot(cat, wout_ref[...])
    shift_hist.wait()
    store_new.wait()


def _ab_sample(x, hist_t, gn, win, pw, ps, lng, lnb, coef, bias, wout, *, layer, nb, nt, past_len):
    rows, d = x.shape
    d_pool = ps.shape[1]
    d_gmlp = lng.shape[1]
    any_spec = pl.BlockSpec(memory_space=pl.ANY)
    small = lambda w: _resident(w.shape)
    return pl.pallas_call(
        functools.partial(_ab_sample_kernel, layer=layer, nb=nb, nt=nt, past_len=past_len),
        grid=(1,),
        in_specs=[_resident((rows, d)), any_spec, small(gn), any_spec, _layer(pw, layer), small(ps), small(lng),
                  small(lnb), small(coef), small(bias), any_spec],
        out_specs=[_resident((rows, d)), any_spec, _resident((rows, d_gmlp))],
        out_shape=[jax.ShapeDtypeStruct((rows, d), F32),
                   jax.ShapeDtypeStruct((POOL_HIST, nb, d_pool), F32),
                   jax.ShapeDtypeStruct((rows, d_gmlp), F32)],
        scratch_shapes=[pltpu.VMEM(win.shape[1:], F32), pltpu.VMEM(wout.shape[1:], F32),
                        pltpu.VMEM((POOL_HIST, nb, d_pool), F32), pltpu.VMEM((nt, nb, d_pool), F32),
                        pltpu.SemaphoreType.DMA((5,))],
        compiler_params=_params(1),
        name="ab_sample",
    )(x, hist_t, gn, win, pw, ps, lng, lnb, coef, bias, wout)


def _glu_project(x, gn, wpw1, d_conv):
    h = _rmsnorm(x, gn)
    z = _dot(h, wpw1)
    return z[:, :d_conv] * jax.nn.sigmoid(z[:, d_conv:])


def _conv_tail(c, lng, lnb, wpw2):
    c = _layernorm(c, lng, lnb)
    return _dot(c * jax.nn.sigmoid(c), wpw2)


def _c_prompt_kernel(x_ref, gn_ref, wpw1_ref, wdw_ref, bdw_ref, lng_ref, lnb_ref, wpw2_ref,
                     y_ref, hist_ref, ext_ref, conv_ref, *, tm):
    j = pl.program_id(1)
    d_conv = bdw_ref.shape[1]
    halo = CONV_HALO
    lead = halo - CONV_HIST
    n_pairs = d_conv // (2 * LANES)

    @pl.when(j == 0)
    def _():
        ext_ref[:, 0:2 * halo, :] = jnp.zeros((n_pairs, 2 * halo, LANES), F32)

    @pl.when(j > 0)
    def _():
        ext_ref[:, 0:2 * halo, :] = ext_ref[:, 2 * tm:2 * (tm + halo), :]

    x = x_ref[...]
    h = _rmsnorm(x, gn_ref[...])
    rb = CONV_ROW_BLOCK
    width = d_conv // GLU_COLUMN_CHUNKS
    for c0 in range(0, d_conv, width):
        glu = _dot(h, wpw1_ref[:, c0:c0 + width]) * jax.nn.sigmoid(_dot(h, wpw1_ref[:, d_conv + c0:d_conv + c0 + width]))
        hist_ref[:, c0:c0 + width] = glu[tm - CONV_HIST:, :]
        blocks = range(c0 // LANES, (c0 + width) // LANES)
        for cb in blocks:
            ext_ref[cb // 2, pl.ds(2 * halo + cb % 2, tm, stride=2), :] = glu[:, cb * LANES - c0:(cb + 1) * LANES - c0]
        for cb in blocks:
            cols = slice(cb * LANES, (cb + 1) * LANES)
            w = [jnp.broadcast_to(wdw_ref[k:k + 1, cols], (rb, LANES)) for k in range(CONV_WIDTH)]
            b = jnp.broadcast_to(bdw_ref[:, cols], (rb, LANES))
            for r in range(0, tm, rb):
                chains = [b] + [None] * (CONV_CHAINS - 1)
                for k in range(CONV_WIDTH):
                    term = w[k] * ext_ref[cb // 2, pl.ds(2 * (lead + r + k) + cb % 2, rb, stride=2), :]
                    c = k % CONV_CHAINS
                    chains[c] = term if chains[c] is None else chains[c] + term
                conv_ref[r:r + rb, cols] = functools.reduce(lambda p, q: p + q, chains)

    y_ref[...] = x + _conv_tail(conv_ref[...], lng_ref[...], lnb_ref[...], wpw2_ref[...])


def _c_prompt(x, gn, wpw1, wdw, bdw, lng, lnb, wpw2, *, layer, batch, seq, tm):
    d = x.shape[1]
    d_conv = bdw.shape[1]
    assert seq % tm == 0 and tm % CONV_ROW_BLOCK == 0 and d_conv % (2 * LANES * GLU_COLUMN_CHUNKS) == 0
    tiles = seq // tm
    weights = (gn, wpw1, wdw, bdw, lng, lnb, wpw2)
    stacked = (wpw1, wpw2)
    return pl.pallas_call(
        functools.partial(_c_prompt_kernel, tm=tm),
        grid=(batch, tiles),
        in_specs=[pl.BlockSpec((tm, d), lambda b, j: (b * tiles + j, 0))]
                 + [_layer(w, layer) if any(w is s for s in stacked) else _resident(w.shape) for w in weights],
        out_specs=[pl.BlockSpec((tm, d), lambda b, j: (b * tiles + j, 0)),
                   pl.BlockSpec((None, CONV_HIST, d_conv), lambda b, j: (b, 0, 0))],
        out_shape=[jax.ShapeDtypeStruct((batch * seq, d), F32),
                   jax.ShapeDtypeStruct((batch, CONV_HIST, d_conv), F32)],
        scratch_shapes=[pltpu.VMEM((d_conv // (2 * LANES), 2 * (CONV_HALO + tm), LANES), F32),
                        pltpu.VMEM((tm, d_conv), F32)],
        compiler_params=_params(2),
        name="c_prompt",
    )(x, *weights)


def _c_sample_kernel(x_ref, hist_hbm, gn_ref, wpw1_hbm, wdw_ref, bdw_ref, lng_ref, lnb_ref, wpw2_hbm,
                     y_ref, nhist_hbm, wpw1_ref, wpw2_ref, hist_ref, glu_ref, conv_ref, sem, *, layer, nb, nt):
    d_conv = bdw_ref.shape[1]
    load_w1 = pltpu.make_async_copy(wpw1_hbm.at[layer], wpw1_ref, sem.at[0])
    load_hist = pltpu.make_async_copy(hist_hbm.at[layer], hist_ref, sem.at[1])
    load_w2 = pltpu.make_async_copy(wpw2_hbm.at[layer], wpw2_ref, sem.at[2])
    shift_hist = pltpu.make_async_copy(hist_ref.at[pl.ds(nt, CONV_HIST - nt)],
                                       nhist_hbm.at[pl.ds(0, CONV_HIST - nt)], sem.at[3])
    store_new = pltpu.make_async_copy(glu_ref, nhist_hbm.at[pl.ds(CONV_HIST - nt, nt)], sem.at[4])
    for cp in (load_w1, load_hist, load_w2):
        cp.start()

    x = x_ref[...]
    load_w1.wait()
    glu = _glu_project(x, gn_ref[...], wpw1_ref[...], d_conv)
    for t in range(nt):
        glu_ref[t] = glu[t * nb:(t + 1) * nb, :]
    store_new.start()

    def ext(k):
        return hist_ref[k] if k < CONV_HIST else glu_ref[k - CONV_HIST]

    load_hist.wait()
    shift_hist.start()
    for t in range(nt):
        acc = jnp.broadcast_to(bdw_ref[...], (nb, d_conv))
        for k in range(CONV_WIDTH):
            acc = acc + wdw_ref[k:k + 1, :] * ext(t + k)
        conv_ref[t * nb:(t + 1) * nb, :] = acc

    load_w2.wait()
    y_ref[...] = x + _conv_tail(conv_ref[...], lng_ref[...], lnb_ref[...], wpw2_ref[...])
    shift_hist.wait()
    store_new.wait()


def _c_sample(x, hist_t, gn, wpw1, wdw, bdw, lng, lnb, wpw2, *, layer, nb, nt):
    rows, d = x.shape
    d_conv = bdw.shape[1]
    any_spec = pl.BlockSpec(memory_space=pl.ANY)
    return pl.pallas_call(
        functools.partial(_c_sample_kernel, layer=layer, nb=nb, nt=nt),
        grid=(1,),
        in_specs=[_resident((rows, d)), any_spec, _resident(gn.shape), any_spec, _resident(wdw.shape),
                  _resident(bdw.shape), _resident(lng.shape), _resident(lnb.shape), any_spec],
        out_specs=[_resident((rows, d)), any_spec],
        out_shape=[jax.ShapeDtypeStruct((rows, d), F32),
                   jax.ShapeDtypeStruct((CONV_HIST, nb, d_conv), F32)],
        scratch_shapes=[pltpu.VMEM(wpw1.shape[1:], F32), pltpu.VMEM(wpw2.shape[1:], F32),
                        pltpu.VMEM((CONV_HIST, nb, d_conv), F32), pltpu.VMEM((nt, nb, d_conv), F32),
                        pltpu.VMEM((rows, d_conv), F32), pltpu.SemaphoreType.DMA((5,))],
        compiler_params=_params(1),
        name="c_sample",
    )(x, hist_t, gn, wpw1, wdw, bdw, lng, lnb, wpw2)


def kernel(x_prompt, x_sample, state_pool, state_conv, norm_mix, norm_ffn, norm_final, ab_w_in, ab_pool_w, ab_pool_scale, ab_ln_g, ab_ln_b, ab_ws, ab_bs, ab_w_out, c_w_pw1, c_w_dw, c_b_dw, c_ln_g, c_ln_b, c_w_pw2, ffn_w1, ffn_w2):
    batch, seq, d = x_prompt.shape
    nb, nt, _ = x_sample.shape
    depth = norm_mix.shape[0]
    past_len = PAST_LEN
    assert past_len % GMLP_CHUNK == 0 and nt <= GMLP_CHUNK

    row = lambda v: v.reshape(1, -1)
    xp = x_prompt.reshape(batch * seq, d)
    xs = jnp.transpose(x_sample, (1, 0, 2)).reshape(nt * nb, d)
    gf = row(norm_final)
    pool_t = jnp.transpose(state_pool, (0, 2, 1, 3))
    conv_t = jnp.transpose(state_conv, (0, 2, 1, 3))

    pool_p, pool_s, conv_p, conv_s, gate_v_s = [], [], [], [], []
    for layer in range(depth):
        i = layer // 2
        gn = row(norm_mix[layer])
        if layer % 2 == 0:
            ps, lng, lnb = row(ab_pool_scale[i]), row(ab_ln_g[i]), row(ab_ln_b[i])
            head = lng.shape[1] // ab_ws.shape[1]
            xp, hp = _ab_prompt(xp, gn, ab_w_in, ab_pool_w, ps, lng, lnb, ab_ws, ab_bs[i].T, ab_w_out,
                                layer=i, batch=batch, seq=seq, tm=AB_ROW_TILE)
            coef = jnp.repeat(jnp.transpose(ab_ws[i][:, :nt, :nt], (1, 2, 0)).reshape(nt * nt, -1), head, axis=1)
            bias = jnp.repeat(ab_bs[i][:, :nt].T, head, axis=1)
            xs, hs, vs = _ab_sample(xs, pool_t, gn, ab_w_in, ab_pool_w, ps, lng, lnb, coef, bias, ab_w_out,
                                    layer=i, nb=nb, nt=nt, past_len=past_len)
            pool_p.append(hp)
            pool_s.append(jnp.transpose(hs, (1, 0, 2)))
            gate_v_s.append(jnp.transpose(vs.reshape(nt, nb, -1), (1, 0, 2)))
        else:
            bdw, lng, lnb = row(c_b_dw[i]), row(c_ln_g[i]), row(c_ln_b[i])
            xp, hp = _c_prompt(xp, gn, c_w_pw1, c_w_dw[i], bdw, lng, lnb, c_w_pw2,
                               layer=i, batch=batch, seq=seq, tm=ROW_TILE)
            xs, hs = _c_sample(xs, conv_t, gn, c_w_pw1, c_w_dw[i], bdw, lng, lnb, c_w_pw2, layer=i, nb=nb, nt=nt)
            conv_p.append(hp)
            conv_s.append(jnp.transpose(hs, (1, 0, 2)))
        gff = row(norm_ffn[layer])
        final = layer == depth - 1
        xp = _ffn(xp, gff, ffn_w1, ffn_w2, gf, layer=layer, final=final, tm=ROW_TILE)
        xs = _ffn(xs, gff, ffn_w1, ffn_w2, gf, layer=layer, final=final, tm=nt * nb)

    y_prompt = xp.reshape(batch, seq, d)
    y_sample = jnp.transpose(xs.reshape(nt, nb, d), (1, 0, 2))
    return (y_prompt, y_sample, jnp.stack(pool_p), jnp.stack(pool_s), jnp.stack(conv_p), jnp.stack(conv_s),
            jnp.stack(gate_v_s))
```

```python
import functools

import jax
import jax.numpy as jnp
from jax import lax
from jax.experimental import pallas as pl
from jax.experimental.pallas import tpu as pltpu

EPS = 1e-6
POOL_WINDOWS = (2, 4, 8, 16)
POOL_HIST = max(POOL_WINDOWS) - 1
GMLP_CHUNK = 128
CONV_WIDTH = 31
CONV_HIST = CONV_WIDTH - 1
PAST_LEN = 16384

LANES = 128
SUBLANES = 8
VMEM_LIMIT_BYTES = 56 * 1024 * 1024
ROW_TILE = 512
AB_ROW_TILE = 1024
C_ROW_TILE = 1024
CONV_ROW_BLOCK = 16
CONV_CHAINS = 2
GLU_COLUMN_CHUNKS = 4
FFN_WEIGHT_CHUNKS = 4
POOL_HALO = 2 * SUBLANES
CONV_HALO = 4 * SUBLANES

F32 = jnp.float32
BF16 = jnp.bfloat16


def _rmsnorm(x, g):
    return x * lax.rsqrt(jnp.mean(x * x, axis=-1, keepdims=True) + EPS) * g


def _layernorm(x, g, b):
    mu = jnp.mean(x, axis=-1, keepdims=True)
    xc = x - mu
    var = jnp.mean(xc * xc, axis=-1, keepdims=True)
    return xc * lax.rsqrt(var + EPS) * g + b


def _gelu(x):
    return 0.5 * x * (1.0 + lax.erf(x * (2.0 ** -0.5)))


def _dot(a, b):
    return jnp.dot(a, b, preferred_element_type=F32)


def _resident(shape):
    zeros = (0,) * len(shape)
    return pl.BlockSpec(shape, lambda *_: zeros, pipeline_mode=pl.Buffered(1))


def _layer(stacked, layer):
    zeros = (0,) * (stacked.ndim - 1)
    return pl.BlockSpec((None,) + stacked.shape[1:], lambda *_: (layer,) + zeros, pipeline_mode=pl.Buffered(1))


def _params(n_grid_dims):
    return pltpu.CompilerParams(dimension_semantics=("arbitrary",) * n_grid_dims,
                                vmem_limit_bytes=VMEM_LIMIT_BYTES)


def _ffn_kernel(x_ref, g_ref, w1_hbm, w2_hbm, gf_ref, o_ref, w1_ref, w2_ref, sem, *, layer, final):
    i = pl.program_id(0)
    dff = w1_ref.shape[1]
    chunk = dff // FFN_WEIGHT_CHUNKS

    def copies(c):
        cols = pl.ds(c * chunk, chunk)
        return (pltpu.make_async_copy(w1_hbm.at[layer, :, cols], w1_ref.at[:, cols], sem.at[0, c]),
                pltpu.make_async_copy(w2_hbm.at[layer, cols, :], w2_ref.at[cols, :], sem.at[1, c]))

    def finish(x, y):
        if final:
            y = _rmsnorm(y, gf_ref[...])
        o_ref[...] = y

    @pl.when(i == 0)
    def _():
        for c in range(FFN_WEIGHT_CHUNKS):
            for cp in copies(c):
                cp.start()
        x = x_ref[...]
        h = _rmsnorm(x, g_ref[...])
        y = x
        for c in range(FFN_WEIGHT_CHUNKS):
            cp1, cp2 = copies(c)
            cp1.wait()
            a = jnp.square(jnp.maximum(_dot(h, w1_ref[:, c * chunk:(c + 1) * chunk]), 0.0))
            cp2.wait()
            y = y + _dot(a, w2_ref[c * chunk:(c + 1) * chunk, :])
        finish(x, y)

    @pl.when(i > 0)
    def _():
        x = x_ref[...]
        h = _rmsnorm(x, g_ref[...])
        a = jnp.square(jnp.maximum(_dot(h, w1_ref[...]), 0.0))
        finish(x, x + _dot(a, w2_ref[...]))


def _ffn(x, g, w1, w2, gf, *, layer, final, tm):
    rows, d = x.shape
    dff = w1.shape[2]
    assert dff % (FFN_WEIGHT_CHUNKS * LANES) == 0
    return pl.pallas_call(
        functools.partial(_ffn_kernel, layer=layer, final=final),
        grid=(rows // tm,),
        in_specs=[pl.BlockSpec((tm, d), lambda i: (i, 0)), _resident((1, d)),
                  pl.BlockSpec(memory_space=pl.ANY), pl.BlockSpec(memory_space=pl.ANY), _resident((1, d))],
        out_specs=pl.BlockSpec((tm, d), lambda i: (i, 0)),
        out_shape=jax.ShapeDtypeStruct((rows, d), F32),
        scratch_shapes=[pltpu.VMEM((d, dff), F32), pltpu.VMEM((dff, d), F32),
                        pltpu.SemaphoreType.DMA((2, FFN_WEIGHT_CHUNKS))],
        compiler_params=_params(1),
        name="ffn",
    )(x, g, w1, w2, gf)


def _ab_project(x, gn, win, lng, lnb, d_pool, d_gmlp):
    h = _rmsnorm(x, gn)
    z = _dot(h, win)
    a = z[:, :d_pool]
    u = _gelu(z[:, d_pool:d_pool + d_gmlp])
    v = _gelu(z[:, d_pool + d_gmlp:])
    return a, u, _layernorm(v, lng, lnb)


def _pool_project(pooled, pw_ref, scale):
    group = pw_ref.shape[1]
    mixed = [_dot(pooled[:, g * group:(g + 1) * group], pw_ref[g])
             for g in range(pw_ref.shape[0])]
    return jnp.concatenate(mixed, axis=1) * scale


def _ab_prompt_kernel(x_ref, gn_ref, win_ref, pw_ref, ps_ref, lng_ref, lnb_ref, ws_ref, bst_ref, wout_ref,
                      y_ref, hist_ref, aext_ref, *, tm):
    j = pl.program_id(1)
    d_pool = ps_ref.shape[1]
    d_gmlp = lng_ref.shape[1]
    n_heads = ws_ref.shape[0]
    head = d_gmlp // n_heads
    group = d_pool // len(POOL_WINDOWS)
    halo = POOL_HALO

    @pl.when(j == 0)
    def _():
        aext_ref[0:halo, :] = jnp.zeros((halo, d_pool), F32)

    @pl.when(j > 0)
    def _():
        aext_ref[0:halo, :] = aext_ref[tm:tm + halo, :]

    x = x_ref[...]
    a, u, v = _ab_project(x, gn_ref[...], win_ref[...], lng_ref[...], lnb_ref[...], d_pool, d_gmlp)
    aext_ref[halo:halo + tm, :] = a
    hist_ref[...] = a[tm - POOL_HIST:, :]

    pos = j * tm + lax.broadcasted_iota(jnp.int32, (tm, 1), 0)
    pooled = []
    for g, w in enumerate(POOL_WINDOWS):
        cols = slice(g * group, (g + 1) * group)
        s = a[:, cols]
        for k in range(1, w):
            s = s + aext_ref[halo - k:halo - k + tm, cols]
        cnt = jnp.minimum(pos + 1, w).astype(F32)
        pooled.append(s / cnt - a[:, cols])
    pool_out = _pool_project(jnp.concatenate(pooled, axis=1), pw_ref, ps_ref[...])

    n_chunks = tm // GMLP_CHUNK
    tril = (lax.broadcasted_iota(jnp.int32, (GMLP_CHUNK, GMLP_CHUNK), 0)
            >= lax.broadcasted_iota(jnp.int32, (GMLP_CHUNK, GMLP_CHUNK), 1))
    mixed = []
    for h in range(n_heads):
        wm = jnp.where(tril, ws_ref[h], 0.0)
        rhs = jnp.concatenate([v[c * GMLP_CHUNK:(c + 1) * GMLP_CHUNK, h * head:(h + 1) * head]
                               for c in range(n_chunks)], axis=1)
        mixed.append(_dot(wm, rhs) + bst_ref[:, h:h + 1])
    gate = jnp.concatenate(
        [jnp.concatenate([mixed[h][:, c * head:(c + 1) * head] for h in range(n_heads)], axis=1)
         for c in range(n_chunks)], axis=0)

    cat = jnp.concatenate([pool_out, u * gate], axis=1)
    y_ref[...] = x + _dot(cat, wout_ref[...])


def _ab_prompt(x, gn, win, pw, ps, lng, lnb, ws, bst, wout, *, layer, batch, seq, tm):
    d = x.shape[1]
    d_pool = ps.shape[1]
    assert tm % GMLP_CHUNK == 0 and seq % tm == 0
    tiles = seq // tm
    weights = (gn, win, pw, ps, lng, lnb, ws, bst, wout)
    stacked = (win, pw, ws, wout)
    return pl.pallas_call(
        functools.partial(_ab_prompt_kernel, tm=tm),
        grid=(batch, tiles),
        in_specs=[pl.BlockSpec((tm, d), lambda b, j: (b * tiles + j, 0))]
                 + [_layer(w, layer) if any(w is s for s in stacked) else _resident(w.shape) for w in weights],
        out_specs=[pl.BlockSpec((tm, d), lambda b, j: (b * tiles + j, 0)),
                   pl.BlockSpec((None, POOL_HIST, d_pool), lambda b, j: (b, 0, 0))],
        out_shape=[jax.ShapeDtypeStruct((batch * seq, d), F32),
                   jax.ShapeDtypeStruct((batch, POOL_HIST, d_pool), F32)],
        scratch_shapes=[pltpu.VMEM((POOL_HALO + tm, d_pool), F32)],
        compiler_params=_params(2),
        name="ab_prompt",
    )(x, *weights)


def _ab_sample_kernel(x_ref, hist_hbm, gn_ref, win_hbm, pw_ref, ps_ref, lng_ref, lnb_ref, coef_ref, bias_ref,
                      wout_hbm, y_ref, nhist_hbm, v_ref, win_ref, wout_ref, hist_ref, a_ref, sem,
                      *, layer, nb, nt, past_len):
    d_pool = ps_ref.shape[1]
    d_gmlp = lng_ref.shape[1]
    group = d_pool // len(POOL_WINDOWS)
    load_win = pltpu.make_async_copy(win_hbm.at[layer], win_ref, sem.at[0])
    load_hist = pltpu.make_async_copy(hist_hbm.at[layer], hist_ref, sem.at[1])
    load_wout = pltpu.make_async_copy(wout_hbm.at[layer], wout_ref, sem.at[2])
    shift_hist = pltpu.make_async_copy(hist_ref.at[pl.ds(nt, POOL_HIST - nt)],
                                       nhist_hbm.at[pl.ds(0, POOL_HIST - nt)], sem.at[3])
    store_new = pltpu.make_async_copy(a_ref, nhist_hbm.at[pl.ds(POOL_HIST - nt, nt)], sem.at[4])
    for cp in (load_win, load_hist, load_wout):
        cp.start()

    x = x_ref[...]
    load_win.wait()
    a, u, v = _ab_project(x, gn_ref[...], win_ref[...], lng_ref[...], lnb_ref[...], d_pool, d_gmlp)
    v_ref[...] = v
    for t in range(nt):
        a_ref[t] = a[t * nb:(t + 1) * nb, :]
    store_new.start()

    def ext(k):
        return hist_ref[k] if k < POOL_HIST else a_ref[k - POOL_HIST]

    load_hist.wait()
    shift_hist.start()
    pooled, gate = [], []
    for t in range(nt):
        groups = []
        for g, w in enumerate(POOL_WINDOWS):
            cols = slice(g * group, (g + 1) * group)
            s = ext(POOL_HIST + t)[:, cols]
            for k in range(1, w):
                s = s + ext(POOL_HIST + t - k)[:, cols]
            cnt = float(min(past_len + t + 1, w))
            groups.append(s / cnt - a[t * nb:(t + 1) * nb, cols])
        pooled.append(jnp.concatenate(groups, axis=1))
        m = bias_ref[t:t + 1, :]
        for s_ in range(t + 1):
            vs = v[s_ * nb:(s_ + 1) * nb, :].astype(BF16).astype(F32)
            m = m + coef_ref[t * nt + s_:t * nt + s_ + 1, :].astype(BF16).astype(F32) * vs
        gate.append(m)
    pool_out = _pool_project(jnp.concatenate(pooled, axis=0), pw_ref, ps_ref[...])
    cat = jnp.concatenate([pool_out, u * jnp.concatenate(gate, axis=0)], axis=1)
    load_wout.wait()
    y_ref[...] = x + _dot(cat, wout_ref[...])
    shift_hist.wait()
    store_new.wait()


def _ab_sample(x, hist_t, gn, win, pw, ps, lng, lnb, coef, bias, wout, *, layer, nb, nt, past_len):
    rows, d = x.shape
    d_pool = ps.shape[1]
    d_gmlp = lng.shape[1]
    any_spec = pl.BlockSpec(memory_space=pl.ANY)
    small = lambda w: _resident(w.shape)
    return pl.pallas_call(
        functools.partial(_ab_sample_kernel, layer=layer, nb=nb, nt=nt, past_len=past_len),
        grid=(1,),
        in_specs=[_resident((rows, d)), any_spec, small(gn), any_spec, _layer(pw, layer), small(ps), small(lng),
                  small(lnb), small(coef), small(bias), any_spec],
        out_specs=[_resident((rows, d)), any_spec, _resident((rows, d_gmlp))],
        out_shape=[jax.ShapeDtypeStruct((rows, d), F32),
                   jax.ShapeDtypeStruct((POOL_HIST, nb, d_pool), F32),
                   jax.ShapeDtypeStruct((rows, d_gmlp), F32)],
        scratch_shapes=[pltpu.VMEM(win.shape[1:], F32), pltpu.VMEM(wout.shape[1:], F32),
                        pltpu.VMEM((POOL_HIST, nb, d_pool), F32), pltpu.VMEM((nt, nb, d_pool), F32),
                        pltpu.SemaphoreType.DMA((5,))],
        compiler_params=_params(1),
        name="ab_sample",
    )(x, hist_t, gn, win, pw, ps, lng, lnb, coef, bias, wout)


def _glu_project(x, gn, wpw1, d_conv):
    h = _rmsnorm(x, gn)
    z = _dot(h, wpw1)
    return z[:, :d_conv] * jax.nn.sigmoid(z[:, d_conv:])


def _conv_tail(c, lng, lnb, wpw2):
    c = _layernorm(c, lng, lnb)
    return _dot(c * jax.nn.sigmoid(c), wpw2)


def _c_prompt_kernel(x_ref, gn_ref, wpw1_ref, wdw_ref, bdw_ref, lng_ref, lnb_ref, wpw2_ref,
                     y_ref, hist_ref, ext_ref, conv_ref, *, tm):
    j = pl.program_id(1)
    d_conv = bdw_ref.shape[1]
    halo = CONV_HALO
    lead = halo - CONV_HIST
    n_pairs = d_conv // (2 * LANES)

    @pl.when(j == 0)
    def _():
        ext_ref[:, 0:2 * halo, :] = jnp.zeros((n_pairs, 2 * halo, LANES), F32)

    @pl.when(j > 0)
    def _():
        ext_ref[:, 0:2 * halo, :] = ext_ref[:, 2 * tm:2 * (tm + halo), :]

    x = x_ref[...]
    h = _rmsnorm(x, gn_ref[...])
    rb = CONV_ROW_BLOCK
    width = d_conv // GLU_COLUMN_CHUNKS
    for c0 in range(0, d_conv, width):
        glu = _dot(h, wpw1_ref[:, c0:c0 + width]) * jax.nn.sigmoid(_dot(h, wpw1_ref[:, d_conv + c0:d_conv + c0 + width]))
        hist_ref[:, c0:c0 + width] = glu[tm - CONV_HIST:, :]
        blocks = range(c0 // LANES, (c0 + width) // LANES)
        for cb in blocks:
            ext_ref[cb // 2, pl.ds(2 * halo + cb % 2, tm, stride=2), :] = glu[:, cb * LANES - c0:(cb + 1) * LANES - c0]
        for cb in blocks:
            cols = slice(cb * LANES, (cb + 1) * LANES)
            w = [jnp.broadcast_to(wdw_ref[k:k + 1, cols], (rb, LANES)) for k in range(CONV_WIDTH)]
            b = jnp.broadcast_to(bdw_ref[:, cols], (rb, LANES))
            for r in range(0, tm, rb):
                chains = [b] + [None] * (CONV_CHAINS - 1)
                for k in range(CONV_WIDTH):
                    term = w[k] * ext_ref[cb // 2, pl.ds(2 * (lead + r + k) + cb % 2, rb, stride=2), :]
                    c = k % CONV_CHAINS
                    chains[c] = term if chains[c] is None else chains[c] + term
                conv_ref[r:r + rb, cols] = functools.reduce(lambda p, q: p + q, chains)

    y_ref[...] = x + _conv_tail(conv_ref[...], lng_ref[...], lnb_ref[...], wpw2_ref[...])


def _c_prompt(x, gn, wpw1, wdw, bdw, lng, lnb, wpw2, *, layer, batch, seq, tm):
    d = x.shape[1]
    d_conv = bdw.shape[1]
    assert seq % tm == 0 and tm % CONV_ROW_BLOCK == 0 and d_conv % (2 * LANES * GLU_COLUMN_CHUNKS) == 0
    tiles = seq // tm
    weights = (gn, wpw1, wdw, bdw, lng, lnb, wpw2)
    stacked = (wpw1, wpw2)
    return pl.pallas_call(
        functools.partial(_c_prompt_kernel, tm=tm),
        grid=(batch, tiles),
        in_specs=[pl.BlockSpec((tm, d), lambda b, j: (b * tiles + j, 0))]
                 + [_layer(w, layer) if any(w is s for s in stacked) else _resident(w.shape) for w in weights],
        out_specs=[pl.BlockSpec((tm, d), lambda b, j: (b * tiles + j, 0)),
                   pl.BlockSpec((None, CONV_HIST, d_conv), lambda b, j: (b, 0, 0))],
        out_shape=[jax.ShapeDtypeStruct((batch * seq, d), F32),
                   jax.ShapeDtypeStruct((batch, CONV_HIST, d_conv), F32)],
        scratch_shapes=[pltpu.VMEM((d_conv // (2 * LANES), 2 * (CONV_HALO + tm), LANES), F32),
                        pltpu.VMEM((tm, d_conv), F32)],
        compiler_params=_params(2),
        name="c_prompt",
    )(x, *weights)


def _c_sample_kernel(x_ref, hist_hbm, gn_ref, wpw1_hbm, wdw_ref, bdw_ref, lng_ref, lnb_ref, wpw2_hbm,
                     y_ref, nhist_hbm, wpw1_ref, wpw2_ref, hist_ref, glu_ref, conv_ref, sem, *, layer, nb, nt):
    d_conv = bdw_ref.shape[1]
    load_w1 = pltpu.make_async_copy(wpw1_hbm.at[layer], wpw1_ref, sem.at[0])
    load_hist = pltpu.make_async_copy(hist_hbm.at[layer], hist_ref, sem.at[1])
    load_w2 = pltpu.make_async_copy(wpw2_hbm.at[layer], wpw2_ref, sem.at[2])
    shift_hist = pltpu.make_async_copy(hist_ref.at[pl.ds(nt, CONV_HIST - nt)],
                                       nhist_hbm.at[pl.ds(0, CONV_HIST - nt)], sem.at[3])
    store_new = pltpu.make_async_copy(glu_ref, nhist_hbm.at[pl.ds(CONV_HIST - nt, nt)], sem.at[4])
    for cp in (load_w1, load_hist, load_w2):
        cp.start()

    x = x_ref[...]
    load_w1.wait()
    glu = _glu_project(x, gn_ref[...], wpw1_ref[...], d_conv)
    for t in range(nt):
        glu_ref[t] = glu[t * nb:(t + 1) * nb, :]
    store_new.start()

    def ext(k):
        return hist_ref[k] if k < CONV_HIST else glu_ref[k - CONV_HIST]

    load_hist.wait()
    shift_hist.start()
    for t in range(nt):
        acc = jnp.broadcast_to(bdw_ref[...], (nb, d_conv))
        for k in range(CONV_WIDTH):
            acc = acc + wdw_ref[k:k + 1, :] * ext(t + k)
        conv_ref[t * nb:(t + 1) * nb, :] = acc

    load_w2.wait()
    y_ref[...] = x + _conv_tail(conv_ref[...], lng_ref[...], lnb_ref[...], wpw2_ref[...])
    shift_hist.wait()
    store_new.wait()


def _c_sample(x, hist_t, gn, wpw1, wdw, bdw, lng, lnb, wpw2, *, layer, nb, nt):
    rows, d = x.shape
    d_conv = bdw.shape[1]
    any_spec = pl.BlockSpec(memory_space=pl.ANY)
    return pl.pallas_call(
        functools.partial(_c_sample_kernel, layer=layer, nb=nb, nt=nt),
        grid=(1,),
        in_specs=[_resident((rows, d)), any_spec, _resident(gn.shape), any_spec, _resident(wdw.shape),
                  _resident(bdw.shape), _resident(lng.shape), _resident(lnb.shape), any_spec],
        out_specs=[_resident((rows, d)), any_spec],
        out_shape=[jax.ShapeDtypeStruct((rows, d), F32),
                   jax.ShapeDtypeStruct((CONV_HIST, nb, d_conv), F32)],
        scratch_shapes=[pltpu.VMEM(wpw1.shape[1:], F32), pltpu.VMEM(wpw2.shape[1:], F32),
                        pltpu.VMEM((CONV_HIST, nb, d_conv), F32), pltpu.VMEM((nt, nb, d_conv), F32),
                        pltpu.VMEM((rows, d_conv), F32), pltpu.SemaphoreType.DMA((5,))],
        compiler_params=_params(1),
        name="c_sample",
    )(x, hist_t, gn, wpw1, wdw, bdw, lng, lnb, wpw2)


def kernel(x_prompt, x_sample, state_pool, state_conv, norm_mix, norm_ffn, norm_final, ab_w_in, ab_pool_w, ab_pool_scale, ab_ln_g, ab_ln_b, ab_ws, ab_bs, ab_w_out, c_w_pw1, c_w_dw, c_b_dw, c_ln_g, c_ln_b, c_w_pw2, ffn_w1, ffn_w2):
    batch, seq, d = x_prompt.shape
    nb, nt, _ = x_sample.shape
    depth = norm_mix.shape[0]
    past_len = PAST_LEN
    assert past_len % GMLP_CHUNK == 0 and nt <= GMLP_CHUNK

    row = lambda v: v.reshape(1, -1)
    xp = x_prompt.reshape(batch * seq, d)
    xs = jnp.transpose(x_sample, (1, 0, 2)).reshape(nt * nb, d)
    gf = row(norm_final)
    pool_t = jnp.transpose(state_pool, (0, 2, 1, 3))
    conv_t = jnp.transpose(state_conv, (0, 2, 1, 3))

    pool_p, pool_s, conv_p, conv_s, gate_v_s = [], [], [], [], []
    for layer in range(depth):
        i = layer // 2
        gn = row(norm_mix[layer])
        if layer % 2 == 0:
            ps, lng, lnb = row(ab_pool_scale[i]), row(ab_ln_g[i]), row(ab_ln_b[i])
            head = lng.shape[1] // ab_ws.shape[1]
            xp, hp = _ab_prompt(xp, gn, ab_w_in, ab_pool_w, ps, lng, lnb, ab_ws, ab_bs[i].T, ab_w_out,
                                layer=i, batch=batch, seq=seq, tm=AB_ROW_TILE)
            coef = jnp.repeat(jnp.transpose(ab_ws[i][:, :nt, :nt], (1, 2, 0)).reshape(nt * nt, -1), head, axis=1)
            bias = jnp.repeat(ab_bs[i][:, :nt].T, head, axis=1)
            xs, hs, vs = _ab_sample(xs, pool_t, gn, ab_w_in, ab_pool_w, ps, lng, lnb, coef, bias, ab_w_out,
                                    layer=i, nb=nb, nt=nt, past_len=past_len)
            pool_p.append(hp)
            pool_s.append(jnp.transpose(hs, (1, 0, 2)))
            gate_v_s.append(jnp.transpose(vs.reshape(nt, nb, -1), (1, 0, 2)))
        else:
            bdw, lng, lnb = row(c_b_dw[i]), row(c_ln_g[i]), row(c_ln_b[i])
            xp, hp = _c_prompt(xp, gn, c_w_pw1, c_w_dw[i], bdw, lng, lnb, c_w_pw2,
                               layer=i, batch=batch, seq=seq, tm=C_ROW_TILE)
            xs, hs = _c_sample(xs, conv_t, gn, c_w_pw1, c_w_dw[i], bdw, lng, lnb, c_w_pw2, layer=i, nb=nb, nt=nt)
            conv_p.append(hp)
            conv_s.append(jnp.transpose(hs, (1, 0, 2)))
        gff = row(norm_ffn[layer])
        final = layer == depth - 1
        xp = _ffn(xp, gff, ffn_w1, ffn_w2, gf, layer=layer, final=final, tm=ROW_TILE)
        xs = _ffn(xs, gff, ffn_w1, ffn_w2, gf, layer=layer, final=final, tm=nt * nb)

    y_prompt = xp.reshape(batch, seq, d)
    y_sample = jnp.transpose(xs.reshape(nt, nb, d), (1, 0, 2))
    return (y_prompt, y_sample, jnp.stack(pool_p), jnp.stack(pool_s), jnp.stack(conv_p), jnp.stack(conv_s),
            jnp.stack(gate_v_s))
```

```python
import functools

import jax
import jax.numpy as jnp
from jax import lax
from jax.experimental import pallas as pl
from jax.experimental.pallas import tpu as pltpu

EPS = 1e-6
POOL_WINDOWS = (2, 4, 8, 16)
POOL_HIST = max(POOL_WINDOWS) - 1
GMLP_CHUNK = 128
CONV_WIDTH = 31
CONV_HIST = CONV_WIDTH - 1
PAST_LEN = 16384

LANES = 128
SUBLANES = 8
VMEM_LIMIT_BYTES = 56 * 1024 * 1024
ROW_TILE = 512
AB_ROW_TILE = 1024
C_ROW_TILE = 1024
CONV_ROW_BLOCK = 16
CONV_CHAINS = 2
GLU_COLUMN_CHUNKS = 4
FFN_WEIGHT_CHUNKS = 8
POOL_HALO = 2 * SUBLANES
CONV_HALO = 4 * SUBLANES

F32 = jnp.float32
BF16 = jnp.bfloat16


def _rmsnorm(x, g):
    return x * lax.rsqrt(jnp.mean(x * x, axis=-1, keepdims=True) + EPS) * g


def _layernorm(x, g, b):
    mu = jnp.mean(x, axis=-1, keepdims=True)
    xc = x - mu
    var = jnp.mean(xc * xc, axis=-1, keepdims=True)
    return xc * lax.rsqrt(var + EPS) * g + b


def _gelu(x):
    return 0.5 * x * (1.0 + lax.erf(x * (2.0 ** -0.5)))


def _dot(a, b):
    return jnp.dot(a, b, preferred_element_type=F32)


def _resident(shape):
    zeros = (0,) * len(shape)
    return pl.BlockSpec(shape, lambda *_: zeros, pipeline_mode=pl.Buffered(1))


def _layer(stacked, layer):
    zeros = (0,) * (stacked.ndim - 1)
    return pl.BlockSpec((None,) + stacked.shape[1:], lambda *_: (layer,) + zeros, pipeline_mode=pl.Buffered(1))


def _params(n_grid_dims):
    return pltpu.CompilerParams(dimension_semantics=("arbitrary",) * n_grid_dims,
                                vmem_limit_bytes=VMEM_LIMIT_BYTES)


def _ffn_kernel(x_ref, g_ref, w1_hbm, w2_hbm, gf_ref, o_ref, w1_ref, w2_ref, sem, *, layer, final):
    i = pl.program_id(0)
    dff = w1_ref.shape[1]
    chunk = dff // FFN_WEIGHT_CHUNKS

    def copies(c):
        cols = pl.ds(c * chunk, chunk)
        return (pltpu.make_async_copy(w1_hbm.at[layer, :, cols], w1_ref.at[:, cols], sem.at[0, c]),
                pltpu.make_async_copy(w2_hbm.at[layer, cols, :], w2_ref.at[cols, :], sem.at[1, c]))

    def finish(x, y):
        if final:
            y = _rmsnorm(y, gf_ref[...])
        o_ref[...] = y

    @pl.when(i == 0)
    def _():
        for c in range(FFN_WEIGHT_CHUNKS):
            for cp in copies(c):
                cp.start()
        x = x_ref[...]
        h = _rmsnorm(x, g_ref[...])
        y = x
        for c in range(FFN_WEIGHT_CHUNKS):
            cp1, cp2 = copies(c)
            cp1.wait()
            a = jnp.square(jnp.maximum(_dot(h, w1_ref[:, c * chunk:(c + 1) * chunk]), 0.0))
            cp2.wait()
            y = y + _dot(a, w2_ref[c * chunk:(c + 1) * chunk, :])
        finish(x, y)

    @pl.when(i > 0)
    def _():
        x = x_ref[...]
        h = _rmsnorm(x, g_ref[...])
        a = jnp.square(jnp.maximum(_dot(h, w1_ref[...]), 0.0))
        finish(x, x + _dot(a, w2_ref[...]))


def _ffn(x, g, w1, w2, gf, *, layer, final, tm):
    rows, d = x.shape
    dff = w1.shape[2]
    assert dff % (FFN_WEIGHT_CHUNKS * LANES) == 0
    return pl.pallas_call(
        functools.partial(_ffn_kernel, layer=layer, final=final),
        grid=(rows // tm,),
        in_specs=[pl.BlockSpec((tm, d), lambda i: (i, 0)), _resident((1, d)),
                  pl.BlockSpec(memory_space=pl.ANY), pl.BlockSpec(memory_space=pl.ANY), _resident((1, d))],
        out_specs=pl.BlockSpec((tm, d), lambda i: (i, 0)),
        out_shape=jax.ShapeDtypeStruct((rows, d), F32),
        scratch_shapes=[pltpu.VMEM((d, dff), F32), pltpu.VMEM((dff, d), F32),
                        pltpu.SemaphoreType.DMA((2, FFN_WEIGHT_CHUNKS))],
        compiler_params=_params(1),
        name="ffn",
    )(x, g, w1, w2, gf)


def _ab_project(x, gn, win, lng, lnb, d_pool, d_gmlp):
    h = _rmsnorm(x, gn)
    z = _dot(h, win)
    a = z[:, :d_pool]
    u = _gelu(z[:, d_pool:d_pool + d_gmlp])
    v = _gelu(z[:, d_pool + d_gmlp:])
    return a, u, _layernorm(v, lng, lnb)


def _pool_project(pooled, pw_ref, scale):
    group = pw_ref.shape[1]
    mixed = [_dot(pooled[:, g * group:(g + 1) * group], pw_ref[g])
             for g in range(pw_ref.shape[0])]
    return jnp.concatenate(mixed, axis=1) * scale


def _ab_prompt_kernel(x_ref, gn_ref, win_ref, pw_ref, ps_ref, lng_ref, lnb_ref, ws_ref, bst_ref, wout_ref,
                      y_ref, hist_ref, aext_ref, *, tm):
    j = pl.program_id(1)
    d_pool = ps_ref.shape[1]
    d_gmlp = lng_ref.shape[1]
    n_heads = ws_ref.shape[0]
    head = d_gmlp // n_heads
    group = d_pool // len(POOL_WINDOWS)
    halo = POOL_HALO

    @pl.when(j == 0)
    def _():
        aext_ref[0:halo, :] = jnp.zeros((halo, d_pool), F32)

    @pl.when(j > 0)
    def _():
        aext_ref[0:halo, :] = aext_ref[tm:tm + halo, :]

    x = x_ref[...]
    a, u, v = _ab_project(x, gn_ref[...], win_ref[...], lng_ref[...], lnb_ref[...], d_pool, d_gmlp)
    aext_ref[halo:halo + tm, :] = a
    hist_ref[...] = a[tm - POOL_HIST:, :]

    pos = j * tm + lax.broadcasted_iota(jnp.int32, (tm, 1), 0)
    pooled = []
    for g, w in enumerate(POOL_WINDOWS):
        cols = slice(g * group, (g + 1) * group)
        s = a[:, cols]
        for k in range(1, w):
            s = s + aext_ref[halo - k:halo - k + tm, cols]
        cnt = jnp.minimum(pos + 1, w).astype(F32)
        pooled.append(s / cnt - a[:, cols])
    pool_out = _pool_project(jnp.concatenate(pooled, axis=1), pw_ref, ps_ref[...])

    n_chunks = tm // GMLP_CHUNK
    tril = (lax.broadcasted_iota(jnp.int32, (GMLP_CHUNK, GMLP_CHUNK), 0)
            >= lax.broadcasted_iota(jnp.int32, (GMLP_CHUNK, GMLP_CHUNK), 1))
    mixed = []
    for h in range(n_heads):
        wm = jnp.where(tril, ws_ref[h], 0.0)
        rhs = jnp.concatenate([v[c * GMLP_CHUNK:(c + 1) * GMLP_CHUNK, h * head:(h + 1) * head]
                               for c in range(n_chunks)], axis=1)
        mixed.append(_dot(wm, rhs) + bst_ref[:, h:h + 1])
    gate = jnp.concatenate(
        [jnp.concatenate([mixed[h][:, c * head:(c + 1) * head] for h in range(n_heads)], axis=1)
         for c in range(n_chunks)], axis=0)

    cat = jnp.concatenate([pool_out, u * gate], axis=1)
    y_ref[...] = x + _dot(cat, wout_ref[...])


def _ab_prompt(x, gn, win, pw, ps, lng, lnb, ws, bst, wout, *, layer, batch, seq, tm):
    d = x.shape[1]
    d_pool = ps.shape[1]
    assert tm % GMLP_CHUNK == 0 and seq % tm == 0
    tiles = seq // tm
    weights = (gn, win, pw, ps, lng, lnb, ws, bst, wout)
    stacked = (win, pw, ws, wout)
    return pl.pallas_call(
        functools.partial(_ab_prompt_kernel, tm=tm),
        grid=(batch, tiles),
        in_specs=[pl.BlockSpec((tm, d), lambda b, j: (b * tiles + j, 0))]
                 + [_layer(w, layer) if any(w is s for s in stacked) else _resident(w.shape) for w in weights],
        out_specs=[pl.BlockSpec((tm, d), lambda b, j: (b * tiles + j, 0)),
                   pl.BlockSpec((None, POOL_HIST, d_pool), lambda b, j: (b, 0, 0))],
        out_shape=[jax.ShapeDtypeStruct((batch * seq, d), F32),
                   jax.ShapeDtypeStruct((batch, POOL_HIST, d_pool), F32)],
        scratch_shapes=[pltpu.VMEM((POOL_HALO + tm, d_pool), F32)],
        compiler_params=_params(2),
        name="ab_prompt",
    )(x, *weights)


def _ab_sample_kernel(x_ref, hist_hbm, gn_ref, win_hbm, pw_ref, ps_ref, lng_ref, lnb_ref, coef_ref, bias_ref,
                      wout_hbm, y_ref, nhist_hbm, v_ref, win_ref, wout_ref, hist_ref, a_ref, sem,
                      *, layer, nb, nt, past_len):
    d_pool = ps_ref.shape[1]
    d_gmlp = lng_ref.shape[1]
    group = d_pool // len(POOL_WINDOWS)
    load_win = pltpu.make_async_copy(win_hbm.at[layer], win_ref, sem.at[0])
    load_hist = pltpu.make_async_copy(hist_hbm.at[layer], hist_ref, sem.at[1])
    load_wout = pltpu.make_async_copy(wout_hbm.at[layer], wout_ref, sem.at[2])
    shift_hist = pltpu.make_async_copy(hist_ref.at[pl.ds(nt, POOL_HIST - nt)],
                                       nhist_hbm.at[pl.ds(0, POOL_HIST - nt)], sem.at[3])
    store_new = pltpu.make_async_copy(a_ref, nhist_hbm.at[pl.ds(POOL_HIST - nt, nt)], sem.at[4])
    for cp in (load_win, load_hist, load_wout):
        cp.start()

    x = x_ref[...]
    load_win.wait()
    a, u, v = _ab_project(x, gn_ref[...], win_ref[...], lng_ref[...], lnb_ref[...], d_pool, d_gmlp)
    v_ref[...] = v
    for t in range(nt):
        a_ref[t] = a[t * nb:(t + 1) * nb, :]
    store_new.start()

    def ext(k):
        return hist_ref[k] if k < POOL_HIST else a_ref[k - POOL_HIST]

    load_hist.wait()
    shift_hist.start()
    pooled, gate = [], []
    for t in range(nt):
        groups = []
        for g, w in enumerate(POOL_WINDOWS):
            cols = slice(g * group, (g + 1) * group)
            s = ext(POOL_HIST + t)[:, cols]
            for k in range(1, w):
                s = s + ext(POOL_HIST + t - k)[:, cols]
            cnt = float(min(past_len + t + 1, w))
            groups.append(s / cnt - a[t * nb:(t + 1) * nb, cols])
        pooled.append(jnp.concatenate(groups, axis=1))
        m = bias_ref[t:t + 1, :]
        for s_ in range(t + 1):
            vs = v[s_ * nb:(s_ + 1) * nb, :].astype(BF16).astype(F32)
            m = m + coef_ref[t * nt + s_:t * nt + s_ + 1, :].astype(BF16).astype(F32) * vs
        gate.append(m)
    pool_out = _pool_project(jnp.concatenate(pooled, axis=0), pw_ref, ps_ref[...])
    cat = jnp.concatenate([pool_out, u * jnp.concatenate(gate, axis=0)], axis=1)
    load_wout.wait()
    y_ref[...] = x + _dot(cat, wout_ref[...])
    shift_hist.wait()
    store_new.wait()


def _ab_sample(x, hist_t, gn, win, pw, ps, lng, lnb, coef, bias, wout, *, layer, nb, nt, past_len):
    rows, d = x.shape
    d_pool = ps.shape[1]
    d_gmlp = lng.shape[1]
    any_spec = pl.BlockSpec(memory_space=pl.ANY)
    small = lambda w: _resident(w.shape)
    return pl.pallas_call(
        functools.partial(_ab_sample_kernel, layer=layer, nb=nb, nt=nt, past_len=past_len),
        grid=(1,),
        in_specs=[_resident((rows, d)), any_spec, small(gn), any_spec, _layer(pw, layer), small(ps), small(lng),
                  small(lnb), small(coef), small(bias), any_spec],
        out_specs=[_resident((rows, d)), any_spec, _resident((rows, d_gmlp))],
        out_shape=[jax.ShapeDtypeStruct((rows, d), F32),
                   jax.ShapeDtypeStruct((POOL_HIST, nb, d_pool), F32),
                   jax.ShapeDtypeStruct((rows, d_gmlp), F32)],
        scratch_shapes=[pltpu.VMEM(win.shape[1:], F32), pltpu.VMEM(wout.shape[1:], F32),
                        pltpu.VMEM((POOL_HIST, nb, d_pool), F32), pltpu.VMEM((nt, nb, d_pool), F32),
                        pltpu.SemaphoreType.DMA((5,))],
        compiler_params=_params(1),
        name="ab_sample",
    )(x, hist_t, gn, win, pw, ps, lng, lnb, coef, bias, wout)


def _glu_project(x, gn, wpw1, d_conv):
    h = _rmsnorm(x, gn)
    z = _dot(h, wpw1)
    return z[:, :d_conv] * jax.nn.sigmoid(z[:, d_conv:])


def _conv_tail(c, lng, lnb, wpw2):
    c = _layernorm(c, lng, lnb)
    return _dot(c * jax.nn.sigmoid(c), wpw2)


def _c_prompt_kernel(x_ref, gn_ref, wpw1_ref, wdw_ref, bdw_ref, lng_ref, lnb_ref, wpw2_ref,
                     y_ref, hist_ref, ext_ref, conv_ref, *, tm):
    j = pl.program_id(1)
    d_conv = bdw_ref.shape[1]
    halo = CONV_HALO
    lead = halo - CONV_HIST
    n_pairs = d_conv // (2 * LANES)

    @pl.when(j == 0)
    def _():
        ext_ref[:, 0:2 * halo, :] = jnp.zeros((n_pairs, 2 * halo, LANES), F32)

    @pl.when(j > 0)
    def _():
        ext_ref[:, 0:2 * halo, :] = ext_ref[:, 2 * tm:2 * (tm + halo), :]

    x = x_ref[...]
    h = _rmsnorm(x, gn_ref[...])
    rb = CONV_ROW_BLOCK
    width = d_conv // GLU_COLUMN_CHUNKS
    for c0 in range(0, d_conv, width):
        glu = _dot(h, wpw1_ref[:, c0:c0 + width]) * jax.nn.sigmoid(_dot(h, wpw1_ref[:, d_conv + c0:d_conv + c0 + width]))
        hist_ref[:, c0:c0 + width] = glu[tm - CONV_HIST:, :]
        blocks = range(c0 // LANES, (c0 + width) // LANES)
        for cb in blocks:
            ext_ref[cb // 2, pl.ds(2 * halo + cb % 2, tm, stride=2), :] = glu[:, cb * LANES - c0:(cb + 1) * LANES - c0]
        for cb in blocks:
            cols = slice(cb * LANES, (cb + 1) * LANES)
            w = [jnp.broadcast_to(wdw_ref[k:k + 1, cols], (rb, LANES)) for k in range(CONV_WIDTH)]
            b = jnp.broadcast_to(bdw_ref[:, cols], (rb, LANES))
            for r in range(0, tm, rb):
                chains = [b] + [None] * (CONV_CHAINS - 1)
                for k in range(CONV_WIDTH):
                    term = w[k] * ext_ref[cb // 2, pl.ds(2 * (lead + r + k) + cb % 2, rb, stride=2), :]
                    c = k % CONV_CHAINS
                    chains[c] = term if chains[c] is None else chains[c] + term
                conv_ref[r:r + rb, cols] = functools.reduce(lambda p, q: p + q, chains)

    y_ref[...] = x + _conv_tail(conv_ref[...], lng_ref[...], lnb_ref[...], wpw2_ref[...])


def _c_prompt(x, gn, wpw1, wdw, bdw, lng, lnb, wpw2, *, layer, batch, seq, tm):
    d = x.shape[1]
    d_conv = bdw.shape[1]
    assert seq % tm == 0 and tm % CONV_ROW_BLOCK == 0 and d_conv % (2 * LANES * GLU_COLUMN_CHUNKS) == 0
    tiles = seq // tm
    weights = (gn, wpw1, wdw, bdw, lng, lnb, wpw2)
    stacked = (wpw1, wpw2)
    return pl.pallas_call(
        functools.partial(_c_prompt_kernel, tm=tm),
        grid=(batch, tiles),
        in_specs=[pl.BlockSpec((tm, d), lambda b, j: (b * tiles + j, 0))]
                 + [_layer(w, layer) if any(w is s for s in stacked) else _resident(w.shape) for w in weights],
        out_specs=[pl.BlockSpec((tm, d), lambda b, j: (b * tiles + j, 0)),
                   pl.BlockSpec((None, CONV_HIST, d_conv), lambda b, j: (b, 0, 0))],
        out_shape=[jax.ShapeDtypeStruct((batch * seq, d), F32),
                   jax.ShapeDtypeStruct((batch, CONV_HIST, d_conv), F32)],
        scratch_shapes=[pltpu.VMEM((d_conv // (2 * LANES), 2 * (CONV_HALO + tm), LANES), F32),
                        pltpu.VMEM((tm, d_conv), F32)],
        compiler_params=_params(2),
        name="c_prompt",
    )(x, *weights)


def _c_sample_kernel(x_ref, hist_hbm, gn_ref, wpw1_hbm, wdw_ref, bdw_ref, lng_ref, lnb_ref, wpw2_hbm,
                     y_ref, nhist_hbm, wpw1_ref, wpw2_ref, hist_ref, glu_ref, conv_ref, sem, *, layer, nb, nt):
    d_conv = bdw_ref.shape[1]
    load_w1 = pltpu.make_async_copy(wpw1_hbm.at[layer], wpw1_ref, sem.at[0])
    load_hist = pltpu.make_async_copy(hist_hbm.at[layer], hist_ref, sem.at[1])
    load_w2 = pltpu.make_async_copy(wpw2_hbm.at[layer], wpw2_ref, sem.at[2])
    shift_hist = pltpu.make_async_copy(hist_ref.at[pl.ds(nt, CONV_HIST - nt)],
                                       nhist_hbm.at[pl.ds(0, CONV_HIST - nt)], sem.at[3])
    store_new = pltpu.make_async_copy(glu_ref, nhist_hbm.at[pl.ds(CONV_HIST - nt, nt)], sem.at[4])
    for cp in (load_w1, load_hist, load_w2):
        cp.start()

    x = x_ref[...]
    load_w1.wait()
    glu = _glu_project(x, gn_ref[...], wpw1_ref[...], d_conv)
    for t in range(nt):
        glu_ref[t] = glu[t * nb:(t + 1) * nb, :]
    store_new.start()

    def ext(k):
        return hist_ref[k] if k < CONV_HIST else glu_ref[k - CONV_HIST]

    load_hist.wait()
    shift_hist.start()
    for t in range(nt):
        acc = jnp.broadcast_to(bdw_ref[...], (nb, d_conv))
        for k in range(CONV_WIDTH):
            acc = acc + wdw_ref[k:k + 1, :] * ext(t + k)
        conv_ref[t * nb:(t + 1) * nb, :] = acc

    load_w2.wait()
    y_ref[...] = x + _conv_tail(conv_ref[...], lng_ref[...], lnb_ref[...], wpw2_ref[...])
    shift_hist.wait()
    store_new.wait()


def _c_sample(x, hist_t, gn, wpw1, wdw, bdw, lng, lnb, wpw2, *, layer, nb, nt):
    rows, d = x.shape
    d_conv = bdw.shape[1]
    any_spec = pl.BlockSpec(memory_space=pl.ANY)
    return pl.pallas_call(
        functools.partial(_c_sample_kernel, layer=layer, nb=nb, nt=nt),
        grid=(1,),
        in_specs=[_resident((rows, d)), any_spec, _resident(gn.shape), any_spec, _resident(wdw.shape),
                  _resident(bdw.shape), _resident(lng.shape), _resident(lnb.shape), any_spec],
        out_specs=[_resident((rows, d)), any_spec],
        out_shape=[jax.ShapeDtypeStruct((rows, d), F32),
                   jax.ShapeDtypeStruct((CONV_HIST, nb, d_conv), F32)],
        scratch_shapes=[pltpu.VMEM(wpw1.shape[1:], F32), pltpu.VMEM(wpw2.shape[1:], F32),
                        pltpu.VMEM((CONV_HIST, nb, d_conv), F32), pltpu.VMEM((nt, nb, d_conv), F32),
                        pltpu.VMEM((rows, d_conv), F32), pltpu.SemaphoreType.DMA((5,))],
        compiler_params=_params(1),
        name="c_sample",
    )(x, hist_t, gn, wpw1, wdw, bdw, lng, lnb, wpw2)


def kernel(x_prompt, x_sample, state_pool, state_conv, norm_mix, norm_ffn, norm_final, ab_w_in, ab_pool_w, ab_pool_scale, ab_ln_g, ab_ln_b, ab_ws, ab_bs, ab_w_out, c_w_pw1, c_w_dw, c_b_dw, c_ln_g, c_ln_b, c_w_pw2, ffn_w1, ffn_w2):
    batch, seq, d = x_prompt.shape
    nb, nt, _ = x_sample.shape
    depth = norm_mix.shape[0]
    past_len = PAST_LEN
    assert past_len % GMLP_CHUNK == 0 and nt <= GMLP_CHUNK

    row = lambda v: v.reshape(1, -1)
    xp = x_prompt.reshape(batch * seq, d)
    xs = jnp.transpose(x_sample, (1, 0, 2)).reshape(nt * nb, d)
    gf = row(norm_final)
    pool_t = jnp.transpose(state_pool, (0, 2, 1, 3))
    conv_t = jnp.transpose(state_conv, (0, 2, 1, 3))

    pool_p, pool_s, conv_p, conv_s, gate_v_s = [], [], [], [], []
    for layer in range(depth):
        i = layer // 2
        gn = row(norm_mix[layer])
        if layer % 2 == 0:
            ps, lng, lnb = row(ab_pool_scale[i]), row(ab_ln_g[i]), row(ab_ln_b[i])
            head = lng.shape[1] // ab_ws.shape[1]
            xp, hp = _ab_prompt(xp, gn, ab_w_in, ab_pool_w, ps, lng, lnb, ab_ws, ab_bs[i].T, ab_w_out,
                                layer=i, batch=batch, seq=seq, tm=AB_ROW_TILE)
            coef = jnp.repeat(jnp.transpose(ab_ws[i][:, :nt, :nt], (1, 2, 0)).reshape(nt * nt, -1), head, axis=1)
            bias = jnp.repeat(ab_bs[i][:, :nt].T, head, axis=1)
            xs, hs, vs = _ab_sample(xs, pool_t, gn, ab_w_in, ab_pool_w, ps, lng, lnb, coef, bias, ab_w_out,
                                    layer=i, nb=nb, nt=nt, past_len=past_len)
            pool_p.append(hp)
            pool_s.append(jnp.transpose(hs, (1, 0, 2)))
            gate_v_s.append(jnp.transpose(vs.reshape(nt, nb, -1), (1, 0, 2)))
        else:
            bdw, lng, lnb = row(c_b_dw[i]), row(c_ln_g[i]), row(c_ln_b[i])
            xp, hp = _c_prompt(xp, gn, c_w_pw1, c_w_dw[i], bdw, lng, lnb, c_w_pw2,
                               layer=i, batch=batch, seq=seq, tm=C_ROW_TILE)
            xs, hs = _c_sample(xs, conv_t, gn, c_w_pw1, c_w_dw[i], bdw, lng, lnb, c_w_pw2, layer=i, nb=nb, nt=nt)
            conv_p.append(hp)
            conv_s.append(jnp.transpose(hs, (1, 0, 2)))
        gff = row(norm_ffn[layer])
        final = layer == depth - 1
        xp = _ffn(xp, gff, ffn_w1, ffn_w2, gf, layer=layer, final=final, tm=ROW_TILE)
        xs = _ffn(xs, gff, ffn_w1, ffn_w2, gf, layer=layer, final=final, tm=nt * nb)

    y_prompt = xp.reshape(batch, seq, d)
    y_sample = jnp.transpose(xs.reshape(nt, nb, d), (1, 0, 2))
    return (y_prompt, y_sample, jnp.stack(pool_p), jnp.stack(pool_s), jnp.stack(conv_p), jnp.stack(conv_s),
            jnp.stack(gate_v_s))
```

```python
import functools

import jax
import jax.numpy as jnp
from jax import lax
from jax.experimental import pallas as pl
from jax.experimental.pallas import tpu as pltpu

EPS = 1e-6
POOL_WINDOWS = (2, 4, 8, 16)
POOL_HIST = max(POOL_WINDOWS) - 1
GMLP_CHUNK = 128
CONV_WIDTH = 31
CONV_HIST = CONV_WIDTH - 1
PAST_LEN = 16384

LANES = 128
SUBLANES = 8
VMEM_LIMIT_BYTES = 56 * 1024 * 1024
ROW_TILE = 512
AB_ROW_TILE = 1024
C_ROW_TILE = 1024
CONV_ROW_BLOCK = 16
CONV_CHAINS = 2
GLU_COLUMN_CHUNKS = 4
FFN_WEIGHT_CHUNKS = 4
POOL_HALO = 2 * SUBLANES
CONV_HALO = 4 * SUBLANES

F32 = jnp.float32
BF16 = jnp.bfloat16


def _rmsnorm(x, g):
    return x * lax.rsqrt(jnp.mean(x * x, axis=-1, keepdims=True) + EPS) * g


def _layernorm(x, g, b):
    mu = jnp.mean(x, axis=-1, keepdims=True)
    xc = x - mu
    var = jnp.mean(xc * xc, axis=-1, keepdims=True)
    return xc * lax.rsqrt(var + EPS) * g + b


def _gelu(x):
    return 0.5 * x * (1.0 + lax.erf(x * (2.0 ** -0.5)))


def _dot(a, b):
    return jnp.dot(a, b, preferred_element_type=F32)


def _resident(shape):
    zeros = (0,) * len(shape)
    return pl.BlockSpec(shape, lambda *_: zeros, pipeline_mode=pl.Buffered(1))


def _layer(stacked, layer):
    zeros = (0,) * (stacked.ndim - 1)
    return pl.BlockSpec((None,) + stacked.shape[1:], lambda *_: (layer,) + zeros, pipeline_mode=pl.Buffered(1))


def _params(n_grid_dims):
    return pltpu.CompilerParams(dimension_semantics=("arbitrary",) * n_grid_dims,
                                vmem_limit_bytes=VMEM_LIMIT_BYTES)


def _ffn_kernel(x_ref, g_ref, w1_hbm, w2_hbm, gf_ref, o_ref, w1_ref, w2_ref, sem, *, layer, final):
    i = pl.program_id(0)
    dff = w1_ref.shape[1]
    chunk = dff // FFN_WEIGHT_CHUNKS

    def copies(c):
        cols = pl.ds(c * chunk, chunk)
        return (pltpu.make_async_copy(w1_hbm.at[layer, :, cols], w1_ref.at[:, cols], sem.at[0, c]),
                pltpu.make_async_copy(w2_hbm.at[layer, cols, :], w2_ref.at[cols, :], sem.at[1, c]))

    def finish(x, y):
        if final:
            y = _rmsnorm(y, gf_ref[...])
        o_ref[...] = y

    @pl.when(i == 0)
    def _():
        for c in range(FFN_WEIGHT_CHUNKS):
            for thread, cp in enumerate(copies(c)):
                cp.start(priority=thread)
        x = x_ref[...]
        h = _rmsnorm(x, g_ref[...])
        y = x
        for c in range(FFN_WEIGHT_CHUNKS):
            cp1, cp2 = copies(c)
            cp1.wait()
            a = jnp.square(jnp.maximum(_dot(h, w1_ref[:, c * chunk:(c + 1) * chunk]), 0.0))
            cp2.wait()
            y = y + _dot(a, w2_ref[c * chunk:(c + 1) * chunk, :])
        finish(x, y)

    @pl.when(i > 0)
    def _():
        x = x_ref[...]
        h = _rmsnorm(x, g_ref[...])
        a = jnp.square(jnp.maximum(_dot(h, w1_ref[...]), 0.0))
        finish(x, x + _dot(a, w2_ref[...]))


def _ffn(x, g, w1, w2, gf, *, layer, final, tm):
    rows, d = x.shape
    dff = w1.shape[2]
    assert dff % (FFN_WEIGHT_CHUNKS * LANES) == 0
    return pl.pallas_call(
        functools.partial(_ffn_kernel, layer=layer, final=final),
        grid=(rows // tm,),
        in_specs=[pl.BlockSpec((tm, d), lambda i: (i, 0)), _resident((1, d)),
                  pl.BlockSpec(memory_space=pl.ANY), pl.BlockSpec(memory_space=pl.ANY), _resident((1, d))],
        out_specs=pl.BlockSpec((tm, d), lambda i: (i, 0)),
        out_shape=jax.ShapeDtypeStruct((rows, d), F32),
        scratch_shapes=[pltpu.VMEM((d, dff), F32), pltpu.VMEM((dff, d), F32),
                        pltpu.SemaphoreType.DMA((2, FFN_WEIGHT_CHUNKS))],
        compiler_params=_params(1),
        name="ffn",
    )(x, g, w1, w2, gf)


def _ab_project(x, gn, win, lng, lnb, d_pool, d_gmlp):
    h = _rmsnorm(x, gn)
    z = _dot(h, win)
    a = z[:, :d_pool]
    u = _gelu(z[:, d_pool:d_pool + d_gmlp])
    v = _gelu(z[:, d_pool + d_gmlp:])
    return a, u, _layernorm(v, lng, lnb)


def _pool_project(pooled, pw_ref, scale):
    group = pw_ref.shape[1]
    mixed = [_dot(pooled[:, g * group:(g + 1) * group], pw_ref[g])
             for g in range(pw_ref.shape[0])]
    return jnp.concatenate(mixed, axis=1) * scale


def _ab_prompt_kernel(x_ref, gn_ref, win_ref, pw_ref, ps_ref, lng_ref, lnb_ref, ws_ref, bst_ref, wout_ref,
                      y_ref, hist_ref, aext_ref, *, tm):
    j = pl.program_id(1)
    d_pool = ps_ref.shape[1]
    d_gmlp = lng_ref.shape[1]
    n_heads = ws_ref.shape[0]
    head = d_gmlp // n_heads
    group = d_pool // len(POOL_WINDOWS)
    halo = POOL_HALO

    @pl.when(j == 0)
    def _():
        aext_ref[0:halo, :] = jnp.zeros((halo, d_pool), F32)

    @pl.when(j > 0)
    def _():
        aext_ref[0:halo, :] = aext_ref[tm:tm + halo, :]

    x = x_ref[...]
    a, u, v = _ab_project(x, gn_ref[...], win_ref[...], lng_ref[...], lnb_ref[...], d_pool, d_gmlp)
    aext_ref[halo:halo + tm, :] = a
    hist_ref[...] = a[tm - POOL_HIST:, :]

    pos = j * tm + lax.broadcasted_iota(jnp.int32, (tm, 1), 0)
    pooled = []
    for g, w in enumerate(POOL_WINDOWS):
        cols = slice(g * group, (g + 1) * group)
        s = a[:, cols]
        for k in range(1, w):
            s = s + aext_ref[halo - k:halo - k + tm, cols]
        cnt = jnp.minimum(pos + 1, w).astype(F32)
        pooled.append(s / cnt - a[:, cols])
    pool_out = _pool_project(jnp.concatenate(pooled, axis=1), pw_ref, ps_ref[...])

    n_chunks = tm // GMLP_CHUNK
    tril = (lax.broadcasted_iota(jnp.int32, (GMLP_CHUNK, GMLP_CHUNK), 0)
            >= lax.broadcasted_iota(jnp.int32, (GMLP_CHUNK, GMLP_CHUNK), 1))
    mixed = []
    for h in range(n_heads):
        wm = jnp.where(tril, ws_ref[h], 0.0)
        rhs = jnp.concatenate([v[c * GMLP_CHUNK:(c + 1) * GMLP_CHUNK, h * head:(h + 1) * head]
                               for c in range(n_chunks)], axis=1)
        mixed.append(_dot(wm, rhs) + bst_ref[:, h:h + 1])
    gate = jnp.concatenate(
        [jnp.concatenate([mixed[h][:, c * head:(c + 1) * head] for h in range(n_heads)], axis=1)
         for c in range(n_chunks)], axis=0)

    cat = jnp.concatenate([pool_out, u * gate], axis=1)
    y_ref[...] = x + _dot(cat, wout_ref[...])


def _ab_prompt(x, gn, win, pw, ps, lng, lnb, ws, bst, wout, *, layer, batch, seq, tm):
    d = x.shape[1]
    d_pool = ps.shape[1]
    assert tm % GMLP_CHUNK == 0 and seq % tm == 0
    tiles = seq // tm
    weights = (gn, win, pw, ps, lng, lnb, ws, bst, wout)
    stacked = (win, pw, ws, wout)
    return pl.pallas_call(
        functools.partial(_ab_prompt_kernel, tm=tm),
        grid=(batch, tiles),
        in_specs=[pl.BlockSpec((tm, d), lambda b, j: (b * tiles + j, 0))]
                 + [_layer(w, layer) if any(w is s for s in stacked) else _resident(w.shape) for w in weights],
        out_specs=[pl.BlockSpec((tm, d), lambda b, j: (b * tiles + j, 0)),
                   pl.BlockSpec((None, POOL_HIST, d_pool), lambda b, j: (b, 0, 0))],
        out_shape=[jax.ShapeDtypeStruct((batch * seq, d), F32),
                   jax.ShapeDtypeStruct((batch, POOL_HIST, d_pool), F32)],
        scratch_shapes=[pltpu.VMEM((POOL_HALO + tm, d_pool), F32)],
        compiler_params=_params(2),
        name="ab_prompt",
    )(x, *weights)


def _ab_sample_kernel(x_ref, hist_hbm, gn_ref, win_hbm, pw_ref, ps_ref, lng_ref, lnb_ref, coef_ref, bias_ref,
                      wout_hbm, y_ref, nhist_hbm, v_ref, win_ref, wout_ref, hist_ref, a_ref, sem,
                      *, layer, nb, nt, past_len):
    d_pool = ps_ref.shape[1]
    d_gmlp = lng_ref.shape[1]
    group = d_pool // len(POOL_WINDOWS)
    load_win = pltpu.make_async_copy(win_hbm.at[layer], win_ref, sem.at[0])
    load_hist = pltpu.make_async_copy(hist_hbm.at[layer], hist_ref, sem.at[1])
    load_wout = pltpu.make_async_copy(wout_hbm.at[layer], wout_ref, sem.at[2])
    shift_hist = pltpu.make_async_copy(hist_ref.at[pl.ds(nt, POOL_HIST - nt)],
                                       nhist_hbm.at[pl.ds(0, POOL_HIST - nt)], sem.at[3])
    store_new = pltpu.make_async_copy(a_ref, nhist_hbm.at[pl.ds(POOL_HIST - nt, nt)], sem.at[4])
    for cp in (load_win, load_hist, load_wout):
        cp.start()

    x = x_ref[...]
    load_win.wait()
    a, u, v = _ab_project(x, gn_ref[...], win_ref[...], lng_ref[...], lnb_ref[...], d_pool, d_gmlp)
    v_ref[...] = v
    for t in range(nt):
        a_ref[t] = a[t * nb:(t + 1) * nb, :]
    store_new.start()

    def ext(k):
        return hist_ref[k] if k < POOL_HIST else a_ref[k - POOL_HIST]

    load_hist.wait()
    shift_hist.start()
    pooled, gate = [], []
    for t in range(nt):
        groups = []
        for g, w in enumerate(POOL_WINDOWS):
            cols = slice(g * group, (g + 1) * group)
            s = ext(POOL_HIST + t)[:, cols]
            for k in range(1, w):
                s = s + ext(POOL_HIST + t - k)[:, cols]
            cnt = float(min(past_len + t + 1, w))
            groups.append(s / cnt - a[t * nb:(t + 1) * nb, cols])
        pooled.append(jnp.concatenate(groups, axis=1))
        m = bias_ref[t:t + 1, :]
        for s_ in range(t + 1):
            vs = v[s_ * nb:(s_ + 1) * nb, :].astype(BF16).astype(F32)
            m = m + coef_ref[t * nt + s_:t * nt + s_ + 1, :].astype(BF16).astype(F32) * vs
        gate.append(m)
    pool_out = _pool_project(jnp.concatenate(pooled, axis=0), pw_ref, ps_ref[...])
    cat = jnp.concatenate([pool_out, u * jnp.concatenate(gate, axis=0)], axis=1)
    load_wout.wait()
    y_ref[...] = x + _dot(cat, wout_ref[...])
    shift_hist.wait()
    store_new.wait()


def _ab_sample(x, hist_t, gn, win, pw, ps, lng, lnb, coef, bias, wout, *, layer, nb, nt, past_len):
    rows, d = x.shape
    d_pool = ps.shape[1]
    d_gmlp = lng.shape[1]
    any_spec = pl.BlockSpec(memory_space=pl.ANY)
    small = lambda w: _resident(w.shape)
    return pl.pallas_call(
        functools.partial(_ab_sample_kernel, layer=layer, nb=nb, nt=nt, past_len=past_len),
        grid=(1,),
        in_specs=[_resident((rows, d)), any_spec, small(gn), any_spec, _layer(pw, layer), small(ps), small(lng),
                  small(lnb), small(coef), small(bias), any_spec],
        out_specs=[_resident((rows, d)), any_spec, _resident((rows, d_gmlp))],
        out_shape=[jax.ShapeDtypeStruct((rows, d), F32),
                   jax.ShapeDtypeStruct((POOL_HIST, nb, d_pool), F32),
                   jax.ShapeDtypeStruct((rows, d_gmlp), F32)],
        scratch_shapes=[pltpu.VMEM(win.shape[1:], F32), pltpu.VMEM(wout.shape[1:], F32),
                        pltpu.VMEM((POOL_HIST, nb, d_pool), F32), pltpu.VMEM((nt, nb, d_pool), F32),
                        pltpu.SemaphoreType.DMA((5,))],
        compiler_params=_params(1),
        name="ab_sample",
    )(x, hist_t, gn, win, pw, ps, lng, lnb, coef, bias, wout)


def _glu_project(x, gn, wpw1, d_conv):
    h = _rmsnorm(x, gn)
    z = _dot(h, wpw1)
    return z[:, :d_conv] * jax.nn.sigmoid(z[:, d_conv:])


def _conv_tail(c, lng, lnb, wpw2):
    c = _layernorm(c, lng, lnb)
    return _dot(c * jax.nn.sigmoid(c), wpw2)


def _c_prompt_kernel(x_ref, gn_ref, wpw1_ref, wdw_ref, bdw_ref, lng_ref, lnb_ref, wpw2_ref,
                     y_ref, hist_ref, ext_ref, conv_ref, *, tm):
    j = pl.program_id(1)
    d_conv = bdw_ref.shape[1]
    halo = CONV_HALO
    lead = halo - CONV_HIST
    n_pairs = d_conv // (2 * LANES)

    @pl.when(j == 0)
    def _():
        ext_ref[:, 0:2 * halo, :] = jnp.zeros((n_pairs, 2 * halo, LANES), F32)

    @pl.when(j > 0)
    def _():
        ext_ref[:, 0:2 * halo, :] = ext_ref[:, 2 * tm:2 * (tm + halo), :]

    x = x_ref[...]
    h = _rmsnorm(x, gn_ref[...])
    rb = CONV_ROW_BLOCK
    width = d_conv // GLU_COLUMN_CHUNKS
    for c0 in range(0, d_conv, width):
        glu = _dot(h, wpw1_ref[:, c0:c0 + width]) * jax.nn.sigmoid(_dot(h, wpw1_ref[:, d_conv + c0:d_conv + c0 + width]))
        hist_ref[:, c0:c0 + width] = glu[tm - CONV_HIST:, :]
        blocks = range(c0 // LANES, (c0 + width) // LANES)
        for cb in blocks:
            ext_ref[cb // 2, pl.ds(2 * halo + cb % 2, tm, stride=2), :] = glu[:, cb * LANES - c0:(cb + 1) * LANES - c0]
        for cb in blocks:
            cols = slice(cb * LANES, (cb + 1) * LANES)
            w = [jnp.broadcast_to(wdw_ref[k:k + 1, cols], (rb, LANES)) for k in range(CONV_WIDTH)]
            b = jnp.broadcast_to(bdw_ref[:, cols], (rb, LANES))
            for r in range(0, tm, rb):
                chains = [b] + [None] * (CONV_CHAINS - 1)
                for k in range(CONV_WIDTH):
                    term = w[k] * ext_ref[cb // 2, pl.ds(2 * (lead + r + k) + cb % 2, rb, stride=2), :]
                    c = k % CONV_CHAINS
                    chains[c] = term if chains[c] is None else chains[c] + term
                conv_ref[r:r + rb, cols] = functools.reduce(lambda p, q: p + q, chains)

    y_ref[...] = x + _conv_tail(conv_ref[...], lng_ref[...], lnb_ref[...], wpw2_ref[...])


def _c_prompt(x, gn, wpw1, wdw, bdw, lng, lnb, wpw2, *, layer, batch, seq, tm):
    d = x.shape[1]
    d_conv = bdw.shape[1]
    assert seq % tm == 0 and tm % CONV_ROW_BLOCK == 0 and d_conv % (2 * LANES * GLU_COLUMN_CHUNKS) == 0
    tiles = seq // tm
    weights = (gn, wpw1, wdw, bdw, lng, lnb, wpw2)
    stacked = (wpw1, wpw2)
    return pl.pallas_call(
        functools.partial(_c_prompt_kernel, tm=tm),
        grid=(batch, tiles),
        in_specs=[pl.BlockSpec((tm, d), lambda b, j: (b * tiles + j, 0))]
                 + [_layer(w, layer) if any(w is s for s in stacked) else _resident(w.shape) for w in weights],
        out_specs=[pl.BlockSpec((tm, d), lambda b, j: (b * tiles + j, 0)),
                   pl.BlockSpec((None, CONV_HIST, d_conv), lambda b, j: (b, 0, 0))],
        out_shape=[jax.ShapeDtypeStruct((batch * seq, d), F32),
                   jax.ShapeDtypeStruct((batch, CONV_HIST, d_conv), F32)],
        scratch_shapes=[pltpu.VMEM((d_conv // (2 * LANES), 2 * (CONV_HALO + tm), LANES), F32),
                        pltpu.VMEM((tm, d_conv), F32)],
        compiler_params=_params(2),
        name="c_prompt",
    )(x, *weights)


def _c_sample_kernel(x_ref, hist_hbm, gn_ref, wpw1_hbm, wdw_ref, bdw_ref, lng_ref, lnb_ref, wpw2_hbm,
                     y_ref, nhist_hbm, wpw1_ref, wpw2_ref, hist_ref, glu_ref, conv_ref, sem, *, layer, nb, nt):
    d_conv = bdw_ref.shape[1]
    load_w1 = pltpu.make_async_copy(wpw1_hbm.at[layer], wpw1_ref, sem.at[0])
    load_hist = pltpu.make_async_copy(hist_hbm.at[layer], hist_ref, sem.at[1])
    load_w2 = pltpu.make_async_copy(wpw2_hbm.at[layer], wpw2_ref, sem.at[2])
    shift_hist = pltpu.make_async_copy(hist_ref.at[pl.ds(nt, CONV_HIST - nt)],
                                       nhist_hbm.at[pl.ds(0, CONV_HIST - nt)], sem.at[3])
    store_new = pltpu.make_async_copy(glu_ref, nhist_hbm.at[pl.ds(CONV_HIST - nt, nt)], sem.at[4])
    for cp in (load_w1, load_hist, load_w2):
        cp.start()

    x = x_ref[...]
    load_w1.wait()
    glu = _glu_project(x, gn_ref[...], wpw1_ref[...], d_conv)
    for t in range(nt):
        glu_ref[t] = glu[t * nb:(t + 1) * nb, :]
    store_new.start()

    def ext(k):
        return hist_ref[k] if k < CONV_HIST else glu_ref[k - CONV_HIST]

    load_hist.wait()
    shift_hist.start()
    for t in range(nt):
        acc = jnp.broadcast_to(bdw_ref[...], (nb, d_conv))
        for k in range(CONV_WIDTH):
            acc = acc + wdw_ref[k:k + 1, :] * ext(t + k)
        conv_ref[t * nb:(t + 1) * nb, :] = acc

    load_w2.wait()
    y_ref[...] = x + _conv_tail(conv_ref[...], lng_ref[...], lnb_ref[...], wpw2_ref[...])
    shift_hist.wait()
    store_new.wait()


def _c_sample(x, hist_t, gn, wpw1, wdw, bdw, lng, lnb, wpw2, *, layer, nb, nt):
    rows, d = x.shape
    d_conv = bdw.shape[1]
    any_spec = pl.BlockSpec(memory_space=pl.ANY)
    return pl.pallas_call(
        functools.partial(_c_sample_kernel, layer=layer, nb=nb, nt=nt),
        grid=(1,),
        in_specs=[_resident((rows, d)), any_spec, _resident(gn.shape), any_spec, _resident(wdw.shape),
                  _resident(bdw.shape), _resident(lng.shape), _resident(lnb.shape), any_spec],
        out_specs=[_resident((rows, d)), any_spec],
        out_shape=[jax.ShapeDtypeStruct((rows, d), F32),
                   jax.ShapeDtypeStruct((CONV_HIST, nb, d_conv), F32)],
        scratch_shapes=[pltpu.VMEM(wpw1.shape[1:], F32), pltpu.VMEM(wpw2.shape[1:], F32),
                        pltpu.VMEM((CONV_HIST, nb, d_conv), F32), pltpu.VMEM((nt, nb, d_conv), F32),
                        pltpu.VMEM((rows, d_conv), F32), pltpu.SemaphoreType.DMA((5,))],
        compiler_params=_params(1),
        name="c_sample",
    )(x, hist_t, gn, wpw1, wdw, bdw, lng, lnb, wpw2)


def kernel(x_prompt, x_sample, state_pool, state_conv, norm_mix, norm_ffn, norm_final, ab_w_in, ab_pool_w, ab_pool_scale, ab_ln_g, ab_ln_b, ab_ws, ab_bs, ab_w_out, c_w_pw1, c_w_dw, c_b_dw, c_ln_g, c_ln_b, c_w_pw2, ffn_w1, ffn_w2):
    batch, seq, d = x_prompt.shape
    nb, nt, _ = x_sample.shape
    depth = norm_mix.shape[0]
    past_len = PAST_LEN
    assert past_len % GMLP_CHUNK == 0 and nt <= GMLP_CHUNK

    row = lambda v: v.reshape(1, -1)
    xp = x_prompt.reshape(batch * seq, d)
    xs = jnp.transpose(x_sample, (1, 0, 2)).reshape(nt * nb, d)
    gf = row(norm_final)
    pool_t = jnp.transpose(state_pool, (0, 2, 1, 3))
    conv_t = jnp.transpose(state_conv, (0, 2, 1, 3))

    pool_p, pool_s, conv_p, conv_s, gate_v_s = [], [], [], [], []
    for layer in range(depth):
        i = layer // 2
        gn = row(norm_mix[layer])
        if layer % 2 == 0:
            ps, lng, lnb = row(ab_pool_scale[i]), row(ab_ln_g[i]), row(ab_ln_b[i])
            head = lng.shape[1] // ab_ws.shape[1]
            xp, hp = _ab_prompt(xp, gn, ab_w_in, ab_pool_w, ps, lng, lnb, ab_ws, ab_bs[i].T, ab_w_out,
                                layer=i, batch=batch, seq=seq, tm=AB_ROW_TILE)
            coef = jnp.repeat(jnp.transpose(ab_ws[i][:, :nt, :nt], (1, 2, 0)).reshape(nt * nt, -1), head, axis=1)
            bias = jnp.repeat(ab_bs[i][:, :nt].T, head, axis=1)
            xs, hs, vs = _ab_sample(xs, pool_t, gn, ab_w_in, ab_pool_w, ps, lng, lnb, coef, bias, ab_w_out,
                                    layer=i, nb=nb, nt=nt, past_len=past_len)
            pool_p.append(hp)
            pool_s.append(jnp.transpose(hs, (1, 0, 2)))
            gate_v_s.append(jnp.transpose(vs.reshape(nt, nb, -1), (1, 0, 2)))
        else:
            bdw, lng, lnb = row(c_b_dw[i]), row(c_ln_g[i]), row(c_ln_b[i])
            xp, hp = _c_prompt(xp, gn, c_w_pw1, c_w_dw[i], bdw, lng, lnb, c_w_pw2,
                               layer=i, batch=batch, seq=seq, tm=C_ROW_TILE)
            xs, hs = _c_sample(xs, conv_t, gn, c_w_pw1, c_w_dw[i], bdw, lng, lnb, c_w_pw2, layer=i, nb=nb, nt=nt)
            conv_p.append(hp)
            conv_s.append(jnp.transpose(hs, (1, 0, 2)))
        gff = row(norm_ffn[layer])
        final = layer == depth - 1
        xp = _ffn(xp, gff, ffn_w1, ffn_w2, gf, layer=layer, final=final, tm=ROW_TILE)
        xs = _ffn(xs, gff, ffn_w1, ffn_w2, gf, layer=layer, final=final, tm=nt * nb)

    y_prompt = xp.reshape(batch, seq, d)
    y_sample = jnp.transpose(xs.reshape(nt, nb, d), (1, 0, 2))
    return (y_prompt, y_sample, jnp.stack(pool_p), jnp.stack(pool_s), jnp.stack(conv_p), jnp.stack(conv_s),
            jnp.stack(gate_v_s))
```
